```python
import math
import jax
import jax.numpy as jnp
from jax import lax
import numpy as np

D_MODEL = 1024
BATCH = 4
SEQ = 4096
DEPTH = 4
DEC_BATCH = 16
DEC_SEQ = 2048
PAST_LEN = 128

GRID_W = 64
HEAD_DIM = 64
D_MIX = D_MODEL
A_HEADS = 4
A_DIM = A_HEADS * HEAD_DIM
A_LORA_W = 32
A_LORA_A = 32
A_LORA_G = 64
B_HEADS = 8
B_DIM = B_HEADS * HEAD_DIM
NA_KH_MAX = 8
NA_KW = 16
C_DIM = D_MIX - A_DIM - B_DIM
C_GROUPS = C_DIM // HEAD_DIM
HY_ORDER = 2
HY_BANDS = 16
HY_EMB = 1 + 2 * HY_BANDS
HY_FF = 64
HY_FAST_DECAY_PCT = 0.3
HY_SLOW_DECAY_PCT = 1.5
HY_TARGET = 1e-2
D_FF = 2816
NORM_EPS = 1e-5
GN_EPS = 64e-5
A_COLS = 3 * A_DIM + 2 * A_LORA_W + 2 * A_LORA_A + A_LORA_G
B_COLS = 3 * B_DIM
C_COLS = (HY_ORDER + 1) * C_DIM
N_PROJ = A_COLS + B_COLS + C_COLS

kernel_name = 'hybrid_bidir_rwkv7_natten2d_hyena_encoder'


def rms_norm(x, g):
    xf = x.astype(jnp.float32)
    y = xf * lax.rsqrt(jnp.mean(xf * xf, axis=-1, keepdims=True) + NORM_EPS)
    return (y * g.astype(jnp.float32)).astype(x.dtype)


def group_rms_norm(x, g, n_groups):
    xf = x.astype(jnp.float32)
    xg = xf.reshape(xf.shape[:-1] + (n_groups, xf.shape[-1] // n_groups))
    y = xg * lax.rsqrt(jnp.mean(xg * xg, axis=-1, keepdims=True) + NORM_EPS)
    return y.reshape(xf.shape) * g.astype(jnp.float32)


def swiglu(x, w_in, w_out):
    gate, up = jnp.split(x @ w_in, 2, axis=-1)
    return (jax.nn.silu(gate) * up) @ w_out


def shift_prev(x):
    return jnp.pad(x, ((0, 0), (1, 0), (0, 0)))[:, :-1]


def shift_next(x):
    return jnp.pad(x, ((0, 0), (0, 1), (0, 0)))[:, 1:]


def wkv7_scan(r, decay, k, v, kk, a, reverse):
    bsz, _, n_h, n = r.shape

    def step(s, inp):
        r_t, w_t, k_t, v_t, kk_t, a_t = inp
        s = (s * w_t[:, :, None, :]
             - jnp.einsum('bhij,bhj->bhi', s, kk_t)[..., None] * (kk_t * a_t)[:, :, None, :]
             + v_t[..., None] * k_t[:, :, None, :])
        return s, jnp.einsum('bhij,bhj->bhi', s, r_t)

    xs = tuple(jnp.swapaxes(t, 0, 1) for t in (r, decay, k, v, kk, a))
    s0 = jnp.zeros((bsz, n_h, n, n), jnp.float32)
    _, o = lax.scan(step, s0, xs, reverse=reverse)
    return jnp.swapaxes(o, 0, 1)


def rwkv7_direction(r, decay, k, v, kk, a, r_k, ln_g, ln_b, reverse):
    o = wkv7_scan(r, decay, k, v, kk, a, reverse)
    mu = jnp.mean(o, axis=-1, keepdims=True)
    var = jnp.mean(jnp.square(o - mu), axis=-1, keepdims=True)
    o = (o - mu) * lax.rsqrt(var + GN_EPS) * ln_g + ln_b
    return o + jnp.sum(r * k * r_k, axis=-1, keepdims=True) * v


def rwkv7_mixer(p, lp):
    bsz, t_len, _ = p.shape
    p = p.astype(jnp.float32)
    p = p + lp['rwkv_mu'] * (0.5 * (shift_prev(p) + shift_next(p)) - p)
    cuts = [A_DIM, 2 * A_DIM, 3 * A_DIM, 3 * A_DIM + 2 * A_LORA_W,
            3 * A_DIM + 2 * A_LORA_W + 2 * A_LORA_A]
    r, k, v, wd, ad, gd = jnp.split(p, cuts, axis=-1)
    wd = wd.reshape(bsz, t_len, 2, A_LORA_W)
    ad = ad.reshape(bsz, t_len, 2, A_LORA_A)
    w_log = -jax.nn.softplus(-(lp['rwkv_w0'] + jnp.einsum('btdr,drc->btdc', jnp.tanh(wd), lp['rwkv_w_up']))) - 0.5
    decay = jnp.exp(-jnp.exp(w_log))
    a = jax.nn.sigmoid(lp['rwkv_a0'] + jnp.einsum('btdr,drc->btdc', ad, lp['rwkv_a_up']))
    g = jax.nn.sigmoid(gd) @ lp['rwkv_g_up']

    def heads(t):
        return t.reshape(t.shape[:-1] + (A_HEADS, HEAD_DIM))

    kk = heads(k * lp['rwkv_k_k'])
    kk = kk / jnp.maximum(jnp.sqrt(jnp.sum(kk * kk, axis=-1, keepdims=True)), 1e-12)
    k_dir = heads(k[:, :, None, :] * (1.0 + (a - 1.0) * lp['rwkv_k_a']))
    decay, a = heads(decay), heads(a)
    r, v = heads(r), heads(v)
    ln_g = lp['rwkv_ln_g'].reshape(A_HEADS, HEAD_DIM)
    ln_b = lp['rwkv_ln_b'].reshape(A_HEADS, HEAD_DIM)
    y_fwd = rwkv7_direction(r, decay[:, :, 0], k_dir[:, :, 0], v, kk, a[:, :, 0],
                            lp['rwkv_r_k'], ln_g, ln_b, False)
    y_bwd = rwkv7_direction(r, decay[:, :, 1], k_dir[:, :, 1], v, kk, a[:, :, 1],
                            lp['rwkv_r_k'], ln_g, ln_b, True)
    return (y_fwd + y_bwd).reshape(bsz, t_len, A_DIM) * g


def neighbourhood_attention(p, rpb):
    bsz, t_len, _ = p.shape
    rows = t_len // GRID_W
    kh = min(NA_KH_MAX, rows)
    q, k, v = jnp.split(p, 3, axis=-1)

    def grid(t):
        return t.reshape(bsz, rows, GRID_W, B_HEADS, HEAD_DIM)

    q = grid(q) * (HEAD_DIM ** -0.5)
    k, v = grid(k), grid(v)
    col = jnp.arange(GRID_W)
    col_start = jnp.clip(col - NA_KW // 2, 0, GRID_W - NA_KW)
    col_ok = (col[None, :] >= col_start[:, None]) & (col[None, :] < col_start[:, None] + NA_KW)
    dx_idx = jnp.clip(col[None, :] - col[:, None] + NA_KW - 1, 0, 2 * NA_KW - 2)

    def row_block(r):
        rs = jnp.clip(r - kh // 2, 0, rows - kh)
        kb = lax.dynamic_slice_in_dim(k, rs, kh, axis=1)
        vb = lax.dynamic_slice_in_dim(v, rs, kh, axis=1)
        qr = lax.dynamic_index_in_dim(q, r, axis=1, keepdims=False)
        dy_idx = rs + jnp.arange(kh) - r + NA_KH_MAX - 1
        bias = rpb[:, dy_idx[None, :, None], dx_idx[:, None, :]].astype(jnp.float32)
        s = jnp.einsum('bqhd,bkwhd->bhqkw', qr, kb).astype(jnp.float32) + bias
        s = jnp.where(col_ok[:, None, :], s, -jnp.inf)
        pr = jax.nn.softmax(s.reshape(bsz, B_HEADS, GRID_W, kh * GRID_W), axis=-1)
        pr = pr.reshape(s.shape).astype(vb.dtype)
        return jnp.einsum('bhqkw,bkwhd->bqhd', pr, vb)

    o = lax.map(row_block, jnp.arange(rows))
    return jnp.moveaxis(o, 0, 1).reshape(bsz, t_len, B_DIM)


def hyena_filters(seq_len, lp):
    f32 = jnp.float32
    t = jnp.linspace(0.0, 1.0, seq_len, dtype=f32)[:, None]
    w = 2.0 * math.pi * jnp.arange(seq_len, dtype=f32)[:, None] / seq_len
    f = jnp.linspace(1e-4, HY_BANDS - 1, HY_BANDS, dtype=f32)[None, :]
    z = jnp.concatenate([t, jnp.cos(f * w), -jnp.sin(f * w)], axis=-1)
    freq = lp['hy_sin_freq'].astype(f32)
    h = jnp.sin(freq * (z @ lp['hy_ff_w1'].astype(f32) + lp['hy_ff_b1'].astype(f32)))
    h = jnp.sin(freq * (h @ lp['hy_ff_w2'].astype(f32) + lp['hy_ff_b2'].astype(f32)))
    h = (h @ lp['hy_ff_w3'].astype(f32)).reshape(seq_len, 2, HY_ORDER, C_DIM)
    max_decay = math.log(HY_TARGET) / HY_FAST_DECAY_PCT
    min_decay = math.log(HY_TARGET) / HY_SLOW_DECAY_PCT
    deltas = jnp.linspace(min_decay, max_decay, C_DIM, dtype=f32)
    h = h * jnp.exp(-t * jnp.abs(deltas))[:, None, None, :]
    h_fwd, h_bwd = h[:, 0], h[:, 1]
    kern = jnp.concatenate([h_fwd, jnp.zeros((1, HY_ORDER, C_DIM), f32), jnp.flip(h_bwd[1:], axis=0)], axis=0)
    return kern / jnp.sum(jnp.abs(kern), axis=0, keepdims=True)


def fft_long_conv(u, kern_fft, seq_len):
    u_fft = jnp.fft.rfft(u, n=2 * seq_len, axis=1)
    return jnp.fft.irfft(u_fft * kern_fft[None], n=2 * seq_len, axis=1)[:, :seq_len]


def hyena_mixer(p, lp):
    seq_len = p.shape[1]
    cw = lp['hy_conv_w']
    p = shift_prev(p) * cw[0] + p * cw[1] + shift_next(p) * cw[2] + lp['hy_conv_b']
    v, x1, x2 = jnp.split(p.astype(jnp.float32), HY_ORDER + 1, axis=-1)
    kern_fft = jnp.fft.rfft(hyena_filters(seq_len, lp), axis=0)
    d_skip = lp['hy_bias'].astype(jnp.float32)
    z = x1 * (fft_long_conv(v, kern_fft[:, 0], seq_len) + v * d_skip[0])
    return x2 * (fft_long_conv(z, kern_fft[:, 1], seq_len) + z * d_skip[1])


def encoder_layer(x, lp):
    x = x + 0.5 * swiglu(rms_norm(x, lp['norm_ffn1']), lp['ffn1_w_in'], lp['ffn1_w_out'])
    h = rms_norm(x, lp['norm_mix'])
    proj = h @ lp['w_in']
    pa, pb, pc = jnp.split(proj, [A_COLS, A_COLS + B_COLS], axis=-1)
    ya = rwkv7_mixer(pa, lp)
    yb = group_rms_norm(neighbourhood_attention(pb, lp['na_rpb']), lp['na_out_norm'], B_HEADS)
    yc = group_rms_norm(hyena_mixer(pc, lp), lp['hy_out_norm'], C_GROUPS)
    mixed = jnp.concatenate([ya, yb, yc], axis=-1).astype(x.dtype)
    x = x + mixed @ lp['w_out']
    x = x + 0.5 * swiglu(rms_norm(x, lp['norm_ffn2']), lp['ffn2_w_in'], lp['ffn2_w_out'])
    return x


def encoder_trunk(x, params, final_norm):
    for layer in range(DEPTH):
        lp = {name: arr[layer] for name, arr in params.items()}
        x = encoder_layer(x, lp)
    return rms_norm(x, final_norm)


def setup_inputs(seed: int = 0) -> dict:
    key = jax.random.key(seed)
    ks = iter(jax.random.split(key, 40))
    f32 = jnp.float32
    L = DEPTH

    def normal(shape, scale):
        return jax.random.normal(next(ks), shape, f32) * scale

    def uniform(shape, lo, hi):
        return jax.random.uniform(next(ks), shape, f32, lo, hi)

    def gain(shape):
        return 1.0 + 0.05 * jax.random.normal(next(ks), shape, f32)

    return {
        'x_prompt': normal((BATCH, SEQ, D_MODEL), 1.0),
        'x_sample': normal((DEC_BATCH, DEC_SEQ, D_MODEL), 1.0),
        'norm_ffn1': gain((L, D_MODEL)),
        'ffn1_w_in': normal((L, D_MODEL, 2 * D_FF), D_MODEL ** -0.5),
        'ffn1_w_out': normal((L, D_FF, D_MODEL), D_FF ** -0.5),
        'norm_mix': gain((L, D_MODEL)),
        'w_in': normal((L, D_MODEL, N_PROJ), D_MODEL ** -0.5),
        'rwkv_mu': uniform((L, A_COLS), 0.0, 1.0),
        'rwkv_w0': uniform((L, 2, A_DIM), -5.0, -0.5),
        'rwkv_w_up': normal((L, 2, A_LORA_W, A_DIM), 0.5 * A_LORA_W ** -0.5),
        'rwkv_a0': normal((L, 2, A_DIM), 0.5),
        'rwkv_a_up': normal((L, 2, A_LORA_A, A_DIM), 0.5 * A_LORA_A ** -0.5),
        'rwkv_g_up': normal((L, A_LORA_G, A_DIM), A_LORA_G ** -0.5),
        'rwkv_k_k': 0.85 + normal((L, A_DIM), 0.05),
        'rwkv_k_a': 1.0 + normal((L, A_DIM), 0.05),
        'rwkv_r_k': normal((L, A_HEADS, HEAD_DIM), 0.1),
        'rwkv_ln_g': gain((L, A_DIM)),
        'rwkv_ln_b': normal((L, A_DIM), 0.02),
        'na_rpb': normal((L, B_HEADS, 2 * NA_KH_MAX - 1, 2 * NA_KW - 1), 0.1),
        'na_out_norm': gain((L, B_DIM)),
        'hy_conv_w': normal((L, 3, C_COLS), 3 ** -0.5),
        'hy_conv_b': normal((L, C_COLS), 0.02),
        'hy_ff_w1': normal((L, HY_EMB, HY_FF), HY_EMB ** -0.5),
        'hy_ff_b1': normal((L, HY_FF), 0.5),
        'hy_ff_w2': normal((L, HY_FF, HY_FF), HY_FF ** -0.5),
        'hy_ff_b2': normal((L, HY_FF), 0.5),
        'hy_ff_w3': normal((L, HY_FF, 2 * HY_ORDER * C_DIM), HY_FF ** -0.5),
        'hy_sin_freq': gain((L, HY_FF)),
        'hy_bias': normal((L, HY_ORDER, C_DIM), 0.1),
        'hy_out_norm': gain((L, C_DIM)),
        'w_out': normal((L, D_MIX, D_MODEL), D_MIX ** -0.5),
        'norm_ffn2': gain((L, D_MODEL)),
        'ffn2_w_in': normal((L, D_MODEL, 2 * D_FF), D_MODEL ** -0.5),
        'ffn2_w_out': normal((L, D_FF, D_MODEL), D_FF ** -0.5),
        'final_norm': gain((D_MODEL,)),
    }


def reference(x_prompt, x_sample, norm_ffn1, ffn1_w_in, ffn1_w_out, norm_mix, w_in,
              rwkv_mu, rwkv_w0, rwkv_w_up, rwkv_a0, rwkv_a_up, rwkv_g_up, rwkv_k_k, rwkv_k_a,
              rwkv_r_k, rwkv_ln_g, rwkv_ln_b, na_rpb, na_out_norm, hy_conv_w, hy_conv_b,
              hy_ff_w1, hy_ff_b1, hy_ff_w2, hy_ff_b2, hy_ff_w3, hy_sin_freq, hy_bias, hy_out_norm,
              w_out, norm_ffn2, ffn2_w_in, ffn2_w_out, final_norm):
    params = {
        'norm_ffn1': norm_ffn1, 'ffn1_w_in': ffn1_w_in, 'ffn1_w_out': ffn1_w_out,
        'norm_mix': norm_mix, 'w_in': w_in,
        'rwkv_mu': rwkv_mu, 'rwkv_w0': rwkv_w0, 'rwkv_w_up': rwkv_w_up, 'rwkv_a0': rwkv_a0,
        'rwkv_a_up': rwkv_a_up, 'rwkv_g_up': rwkv_g_up, 'rwkv_k_k': rwkv_k_k, 'rwkv_k_a': rwkv_k_a,
        'rwkv_r_k': rwkv_r_k, 'rwkv_ln_g': rwkv_ln_g, 'rwkv_ln_b': rwkv_ln_b,
        'na_rpb': na_rpb, 'na_out_norm': na_out_norm,
        'hy_conv_w': hy_conv_w, 'hy_conv_b': hy_conv_b, 'hy_ff_w1': hy_ff_w1, 'hy_ff_b1': hy_ff_b1,
        'hy_ff_w2': hy_ff_w2, 'hy_ff_b2': hy_ff_b2, 'hy_ff_w3': hy_ff_w3, 'hy_sin_freq': hy_sin_freq,
        'hy_bias': hy_bias, 'hy_out_norm': hy_out_norm,
        'w_out': w_out, 'norm_ffn2': norm_ffn2, 'ffn2_w_in': ffn2_w_in, 'ffn2_w_out': ffn2_w_out,
    }
    y_prompt = encoder_trunk(x_prompt, params, final_norm)
    y_sample = encoder_trunk(x_sample, params, final_norm)
    return (y_prompt, y_sample)
```

```python
import functools
import math

import numpy as np
import jax
import jax.numpy as jnp
from jax import lax
from jax.experimental import pallas as pl
from jax.experimental.pallas import tpu as pltpu

F32 = jnp.float32
BF16 = jnp.bfloat16

D_MODEL = 1024
DEPTH = 4
GRID_W = 64
HEAD_DIM = 64
A_HEADS = 4
A_DIM = A_HEADS * HEAD_DIM
A_LORA_W = 32
A_LORA_A = 32
A_LORA_G = 64
B_HEADS = 8
B_DIM = B_HEADS * HEAD_DIM
NA_KH_MAX = 8
NA_KW = 16
C_DIM = D_MODEL - A_DIM - B_DIM
C_GROUPS = C_DIM // HEAD_DIM
HY_ORDER = 2
HY_BANDS = 16
HY_EMB = 1 + 2 * HY_BANDS
HY_FF = 64
HY_FAST_DECAY_PCT = 0.3
HY_SLOW_DECAY_PCT = 1.5
HY_TARGET = 1e-2
D_FF = 2816
NORM_EPS = 1e-5
GN_EPS = 64e-5
A_COLS = 3 * A_DIM + 2 * A_LORA_W + 2 * A_LORA_A + A_LORA_G
A_PAD = 1024
B_COLS = 3 * B_DIM
C_COLS = (HY_ORDER + 1) * C_DIM

SUBLANES = 8
VMEM_LIMIT = 56 * 1024 * 1024
WKV_CHUNK = 64
NEG_BIG = -1e30


def _cparams(sem):
    return pltpu.CompilerParams(dimension_semantics=sem, vmem_limit_bytes=VMEM_LIMIT)


def _const_spec(shape):
    nd = len(shape)
    return pl.BlockSpec(shape, lambda *_: (0,) * nd, pipeline_mode=pl.Buffered(1))


def _dot(a, b):
    return jnp.dot(a.astype(BF16), b.astype(BF16), preferred_element_type=F32)


def _dot_nt(a, b):
    return lax.dot_general(a.astype(BF16), b.astype(BF16), (((1,), (1,)), ((), ())),
                           preferred_element_type=F32)


def _dot_tn(a, b):
    return lax.dot_general(a.astype(BF16), b.astype(BF16), (((0,), (0,)), ((), ())),
                           preferred_element_type=F32)


def _split(a):
    hi = a.astype(BF16)
    lo = (a - hi.astype(F32)).astype(BF16)
    return hi, lo


def _dot_lhs2(a, b_bf16):
    hi, lo = _split(a)
    return (jnp.dot(hi, b_bf16, preferred_element_type=F32)
            + jnp.dot(lo, b_bf16, preferred_element_type=F32))


def _dot3(a, b):
    ah, al = _split(a)
    bh, bl = _split(b)
    return (jnp.dot(ah, bh, preferred_element_type=F32)
            + jnp.dot(al, bh, preferred_element_type=F32)
            + jnp.dot(ah, bl, preferred_element_type=F32))


def _rms(x, g):
    return x * lax.rsqrt(jnp.mean(x * x, axis=-1, keepdims=True) + NORM_EPS) * g


def _group_mat(n, group, value):
    idx = np.arange(n) // group
    return jnp.asarray((idx[:, None] == idx[None, :]).astype(np.float32) * value, dtype=BF16)


def _ffn_body(*refs, n_chunks, with_mix, with_final):
    it = iter(refs)
    x_ref = next(it)
    if with_mix:
        yf_ref, yb_ref, g_ref, nb_ref, yc_ref, wo_ref, cn_ref, pavg_ref = (next(it) for _ in range(8))
    ng_ref, win_ref, wout_ref = next(it), next(it), next(it)
    if with_final:
        fn_ref = next(it)
    o_ref = next(it)

    x = x_ref[...]
    if with_mix:
        ya = (yf_ref[...] + yb_ref[...]) * g_ref[...]
        yc = yc_ref[...]
        ms = _dot_lhs2(yc * yc, pavg_ref[...])
        yc = yc * lax.rsqrt(ms + NORM_EPS) * cn_ref[...]
        x = (x + _dot(ya, wo_ref[0:A_DIM, :]) + _dot(nb_ref[...], wo_ref[A_DIM:A_DIM + B_DIM, :])
             + _dot(yc, wo_ref[A_DIM + B_DIM:, :]))
    h = _rms(x, ng_ref[...]).astype(BF16)
    cw = D_FF // n_chunks
    acc = jnp.zeros_like(x)
    for c in range(n_chunks):
        gate = jnp.dot(h, win_ref[:, c * cw:(c + 1) * cw], preferred_element_type=F32)
        up = jnp.dot(h, win_ref[:, D_FF + c * cw:D_FF + (c + 1) * cw], preferred_element_type=F32)
        act = (gate * jax.nn.sigmoid(gate) * up).astype(BF16)
        acc = acc + jnp.dot(act, wout_ref[c * cw:(c + 1) * cw, :], preferred_element_type=F32)
    y = x + 0.5 * acc
    if with_final:
        y = _rms(y, fn_ref[...])
    o_ref[...] = y


def _ffn(x, norm_g, w_in, w_out, *, mix=None, final_g=None, tm=512, n_chunks=2):
    n = x.shape[0]
    tok = lambda w: pl.BlockSpec((tm, w), lambda i: (i, 0))
    args, specs = [x], [tok(D_MODEL)]
    if mix is not None:
        yf, yb, g, nb, yc, wo, cn = mix
        args += [yf, yb, g, nb, yc, wo, cn.reshape(1, C_DIM), _group_mat(C_DIM, HEAD_DIM, 1.0 / HEAD_DIM)]
        specs += [tok(A_DIM), tok(A_DIM), tok(A_DIM), tok(B_DIM), tok(C_DIM),
                  _const_spec((D_MODEL, D_MODEL)), _const_spec((1, C_DIM)), _const_spec((C_DIM, C_DIM))]
    args += [norm_g.reshape(1, D_MODEL), w_in, w_out]
    specs += [_const_spec((1, D_MODEL)), _const_spec((D_MODEL, 2 * D_FF)), _const_spec((D_FF, D_MODEL))]
    if final_g is not None:
        args.append(final_g.reshape(1, D_MODEL))
        specs.append(_const_spec((1, D_MODEL)))
    body = functools.partial(_ffn_body, n_chunks=n_chunks, with_mix=mix is not None,
                             with_final=final_g is not None)
    return pl.pallas_call(
        body, grid=(n // tm,), in_specs=specs, out_specs=tok(D_MODEL),
        out_shape=jax.ShapeDtypeStruct((n, D_MODEL), F32),
        compiler_params=_cparams(("parallel",)), name="ffn")(*args)


def _proj_body(x_ref, g_ref, w_ref, pa_ref, pb_ref, pc_ref):
    h = _rms(x_ref[...], g_ref[...]).astype(BF16)
    pa_ref[...] = jnp.dot(h, w_ref[:, 0:A_PAD], preferred_element_type=F32)
    pb_ref[...] = jnp.dot(h, w_ref[:, A_PAD:A_PAD + B_COLS], preferred_element_type=F32).astype(BF16)
    pc_ref[...] = jnp.dot(h, w_ref[:, A_PAD + B_COLS:], preferred_element_type=F32)


def _proj(x, norm_g, w_pad, *, tm=512):
    n = x.shape[0]
    tok = lambda w: pl.BlockSpec((tm, w), lambda i: (i, 0))
    wcols = A_PAD + B_COLS + C_COLS
    return pl.pallas_call(
        _proj_body, grid=(n // tm,),
        in_specs=[tok(D_MODEL), _const_spec((1, D_MODEL)), _const_spec((D_MODEL, wcols))],
        out_specs=[tok(A_PAD), tok(B_COLS), tok(C_COLS)],
        out_shape=[jax.ShapeDtypeStruct((n, A_PAD), F32), jax.ShapeDtypeStruct((n, B_COLS), BF16),
                   jax.ShapeDtypeStruct((n, C_COLS), F32)],
        compiler_params=_cparams(("parallel",)), name="proj")(x, norm_g.reshape(1, D_MODEL), w_pad)


def _halo_specs(tm, width, n_tok):
    per = tm // SUBLANES
    last = n_tok // SUBLANES - 1
    prev = pl.BlockSpec((SUBLANES, width), lambda i: (jnp.maximum(i * per - 1, 0), 0))
    nxt = pl.BlockSpec((SUBLANES, width), lambda i: (jnp.minimum((i + 1) * per, last), 0))
    return prev, nxt


def _shifted(p, prev_ref, next_ref, blocks_per_seq):
    tm = p.shape[0]
    j = pl.program_id(0) % blocks_per_seq
    hp = jnp.where(j == 0, 0.0, prev_ref[SUBLANES - 1:SUBLANES, :])
    hn = jnp.where(j == blocks_per_seq - 1, 0.0, next_ref[0:1, :])
    row = lax.broadcasted_iota(jnp.int32, p.shape, 0)
    prev = jnp.where(row == 0, hp, pltpu.roll(p, 1, axis=0))
    nxt = jnp.where(row == tm - 1, hn, pltpu.roll(p, tm - 1, axis=0))
    return prev, nxt


def _rwkv_prep_body(p_ref, pp_ref, pn_ref, mu_ref, wlo_ref, w0a0_ref, gup_ref, kk_ref, ka_ref, rk_ref,
                    hsum_ref, r_o, v_o, kk_o, g_o, lw0_o, lw1_o, kd0_o, kd1_o, b0_o, b1_o, bv0_o, bv1_o,
                    *, blocks_per_seq):
    p = p_ref[...]
    prev, nxt = _shifted(p, pp_ref, pn_ref, blocks_per_seq)
    p = p + mu_ref[...] * (0.5 * (prev + nxt) - p)
    r = p[:, 0:A_DIM]
    k = p[:, A_DIM:2 * A_DIM]
    v = p[:, 2 * A_DIM:3 * A_DIM]
    lo = p[:, 3 * A_DIM:3 * A_DIM + 128]
    gd = p[:, 3 * A_DIM + 128:3 * A_DIM + 256]
    lane = lax.broadcasted_iota(jnp.int32, lo.shape, 1)
    lo = jnp.where(lane < 2 * A_LORA_W, jnp.tanh(lo), lo)
    wa = _dot3(lo, wlo_ref[...]) + w0a0_ref[...]
    g = _dot3(jax.nn.sigmoid(gd), gup_ref[...])
    hsum = hsum_ref[...]
    kkv = k * kk_ref[...]
    nrm = jnp.sqrt(_dot_lhs2(kkv * kkv, hsum))
    kkv = kkv / jnp.maximum(nrm, 1e-12)
    r_o[...] = r
    v_o[...] = v
    kk_o[...] = kkv
    g_o[...] = g
    for d, (lw_o, kd_o, b_o, bv_o) in enumerate(((lw0_o, kd0_o, b0_o, bv0_o), (lw1_o, kd1_o, b1_o, bv1_o))):
        w_log = -jax.nn.softplus(-wa[:, d * A_DIM:(d + 1) * A_DIM]) - 0.5
        a = jax.nn.sigmoid(wa[:, (2 + d) * A_DIM:(3 + d) * A_DIM])
        kd = k * (1.0 + (a - 1.0) * ka_ref[...])
        lw_o[...] = -jnp.exp(w_log)
        kd_o[...] = kd
        b_o[...] = kkv * a
        bv_o[...] = _dot_lhs2(r * kd * rk_ref[...], hsum) * v


def _rwkv_prep(pa, lp, *, seq_len, tm=512):
    n = pa.shape[0]
    tok = lambda w: pl.BlockSpec((tm, w), lambda i: (i, 0))
    prev, nxt = _halo_specs(tm, A_PAD, n)
    mu = jnp.pad(lp['rwkv_mu'], (0, A_PAD - A_COLS)).reshape(1, A_PAD)
    wlo = jnp.zeros((128, 4 * A_DIM), F32)
    for d in range(2):
        wlo = wlo.at[d * A_LORA_W:(d + 1) * A_LORA_W, d * A_DIM:(d + 1) * A_DIM].set(lp['rwkv_w_up'][d])
        wlo = wlo.at[64 + d * A_LORA_A:64 + (d + 1) * A_LORA_A, (2 + d) * A_DIM:(3 + d) * A_DIM].set(lp['rwkv_a_up'][d])
    w0a0 = jnp.concatenate([lp['rwkv_w0'].reshape(-1), lp['rwkv_a0'].reshape(-1)]).reshape(1, 4 * A_DIM)
    gup = jnp.pad(lp['rwkv_g_up'], ((0, 128 - A_LORA_G), (0, 0)))
    consts = [mu, wlo, w0a0, gup, lp['rwkv_k_k'].reshape(1, A_DIM), lp['rwkv_k_a'].reshape(1, A_DIM),
              lp['rwkv_r_k'].reshape(1, A_DIM), _group_mat(A_DIM, HEAD_DIM, 1.0)]
    body = functools.partial(_rwkv_prep_body, blocks_per_seq=seq_len // tm)
    return pl.pallas_call(
        body, grid=(n // tm,),
        in_specs=[tok(A_PAD), prev, nxt] + [_const_spec(c.shape) for c in consts],
        out_specs=[tok(A_DIM)] * 12,
        out_shape=[jax.ShapeDtypeStruct((n, A_DIM), F32)] * 12,
        compiler_params=_cparams(("parallel",)), name="rwkv_prep")(pa, pa, pa, *consts)


def _wkv_direction(r, v, kk, lw, kd, b, bv, s_ref, lng, lnb, pavg, o_ref, reverse):
    tb = r.shape[0]
    c_len = WKV_CHUNK
    nc = tb // c_len
    pos = lax.broadcasted_iota(jnp.int32, (tb, A_DIM), 0) % c_len
    cum = lw
    sh = 1
    while sh < c_len:
        if reverse:
            cum = cum + jnp.where(pos < c_len - sh, pltpu.roll(cum, tb - sh, axis=0), 0.0)
        else:
            cum = cum + jnp.where(pos >= sh, pltpu.roll(cum, sh, axis=0), 0.0)
        sh *= 2
    e_neg = jnp.exp(-cum)
    rq_all = r * jnp.exp(cum)
    kq_all = kk * jnp.exp(cum - lw)
    kbar_all = kd * e_neg
    bbar_all = b * e_neg

    t_idx = lax.broadcasted_iota(jnp.int32, (c_len, A_DIM), 0)
    s_idx = lax.broadcasted_iota(jnp.int32, (c_len, A_DIM), 1) % c_len
    if reverse:
        incl, strict = s_idx >= t_idx, s_idx > t_idx
    else:
        incl, strict = s_idx <= t_idx, s_idx < t_idx
    eye = (s_idx == t_idx).astype(F32)
    bd_rows = lax.broadcasted_iota(jnp.int32, (A_DIM, A_DIM), 0) // HEAD_DIM
    bd_cols = lax.broadcasted_iota(jnp.int32, (A_DIM, A_DIM), 1) // HEAD_DIM
    bd_mask = bd_rows == bd_cols

    def stack(x):
        hi, lo = _split(x)
        zero = jnp.zeros((A_DIM, A_DIM), BF16)
        return (jnp.where(bd_mask, jnp.concatenate([hi] * A_HEADS, axis=0), zero),
                jnp.where(bd_mask, jnp.concatenate([lo] * A_HEADS, axis=0), zero))

    def mm(a, b_hl, nt=False):
        bh, bl = b_hl
        ah, al = _split(a)
        m = a.shape[0]
        dg = _dot_nt if nt else _dot
        p = dg(jnp.concatenate([ah, al], axis=0), bh)
        return p[0:m] + p[m:] + dg(ah, bl)

    order = range(nc - 1, -1, -1) if reverse else range(nc)
    for c in order:
        sl = slice(c * c_len, (c + 1) * c_len)
        edge = c * c_len if reverse else (c + 1) * c_len - 1
        e_end = jnp.exp(cum[edge:edge + 1, :])
        rq, kq, kbar, bbar, vc = rq_all[sl], kq_all[sl], kbar_all[sl], bbar_all[sl], v[sl]
        kst, bst = stack(kbar), stack(bbar)
        acat = mm(jnp.concatenate([rq, kq], axis=0),
                  (jnp.concatenate([kst[0], bst[0]], axis=0), jnp.concatenate([kst[1], bst[1]], axis=0)),
                  nt=True)
        a_rk = jnp.where(incl, acat[0:c_len, 0:A_DIM], 0.0)
        a_rb = jnp.where(incl, acat[0:c_len, A_DIM:], 0.0)
        a_kk = jnp.where(strict, acat[c_len:, 0:A_DIM], 0.0)
        a_kb = jnp.where(strict, acat[c_len:, A_DIM:], 0.0)
        half = 1
        tinv = eye
        while half < c_len:
            in_block = (t_idx // (2 * half)) == (s_idx // (2 * half))
            if reverse:
                off = in_block & (t_idx % (2 * half) < half) & (s_idx % (2 * half) >= half)
            else:
                off = in_block & (t_idx % (2 * half) >= half) & (s_idx % (2 * half) < half)
            a_off = jnp.where(off, a_kb, 0.0)
            if half == 1:
                tinv = eye - a_off
            else:
                tinv = tinv - mm(mm(tinv, stack(a_off)), stack(tinv))
            half *= 2
        v_st = stack(vc)
        w_t = mm(tinv, stack(kq))
        u0 = mm(tinv, stack(mm(a_kk, v_st)))
        o_intra = mm(a_rk, v_st)
        s_old = s_ref[...]
        x = mm(jnp.concatenate([rq, w_t], axis=0), _split(s_old), nt=True)
        u = -(x[c_len:] + u0)
        o = x[0:c_len] + o_intra + mm(a_rb, stack(u))
        vu_h, vu_l = _split(jnp.concatenate([vc, u], axis=0))
        kb_h, kb_l = _split(jnp.concatenate([kbar * e_end, bbar * e_end], axis=0))
        ds = _dot_tn(jnp.concatenate([vu_h, vu_l, vu_h], axis=0), jnp.concatenate([kb_h, kb_h, kb_l], axis=0))
        s_ref[...] = s_old * e_end + jnp.where(bd_mask, ds, 0.0)
        mu = _dot_lhs2(o, pavg)
        dev = o - mu
        var = _dot_lhs2(dev * dev, pavg)
        o_ref[sl, :] = dev * lax.rsqrt(var + GN_EPS) * lng + lnb + bv[sl]


def _wkv_body(rf, vf, kkf, lwf, kdf, bf, bvf, rb, vb, kkb, lwb, kdb, bb, bvb, lng_ref, lnb_ref, pavg_ref,
              of_ref, ob_ref, sf_ref, sb_ref):
    @pl.when(pl.program_id(1) == 0)
    def _():
        sf_ref[...] = jnp.zeros_like(sf_ref)
        sb_ref[...] = jnp.zeros_like(sb_ref)

    lng, lnb, pavg = lng_ref[...], lnb_ref[...], pavg_ref[...]
    _wkv_direction(rf[...], vf[...], kkf[...], lwf[...], kdf[...], bf[...], bvf[...], sf_ref, lng, lnb, pavg,
                   of_ref, False)
    _wkv_direction(rb[...], vb[...], kkb[...], lwb[...], kdb[...], bb[...], bvb[...], sb_ref, lng, lnb, pavg,
                   ob_ref, True)


def _wkv(prep, lp, *, batch, seq_len, tb=256):
    r, v, kk, _, lw0, lw1, kd0, kd1, b0, b1, bv0, bv1 = prep
    n = r.shape[0]
    nblk = seq_len // tb
    fwd = pl.BlockSpec((tb, A_DIM), lambda bi, j: (bi * nblk + j, 0))
    bwd = pl.BlockSpec((tb, A_DIM), lambda bi, j: (bi * nblk + nblk - 1 - j, 0))
    consts = [lp['rwkv_ln_g'].reshape(1, A_DIM), lp['rwkv_ln_b'].reshape(1, A_DIM),
              _group_mat(A_DIM, HEAD_DIM, 1.0 / HEAD_DIM)]
    cspec = lambda shape: pl.BlockSpec(shape, lambda bi, j: (0, 0), pipeline_mode=pl.Buffered(1))
    return pl.pallas_call(
        _wkv_body, grid=(batch, nblk),
        in_specs=[fwd] * 7 + [bwd] * 7 + [cspec(c.shape) for c in consts],
        out_specs=[fwd, bwd],
        out_shape=[jax.ShapeDtypeStruct((n, A_DIM), F32)] * 2,
        scratch_shapes=[pltpu.VMEM((A_DIM, A_DIM), F32), pltpu.VMEM((A_DIM, A_DIM), F32)],
        compiler_params=_cparams(("parallel", "arbitrary")), name="wkv")(
            r, v, kk, lw0, kd0, b0, bv0, r, v, kk, lw1, kd1, b1, bv1, *consts)


def _na_tables(rpb, rows):
    kh = min(NA_KH_MAX, rows)
    col = np.arange(GRID_W)
    col_start = np.clip(col - NA_KW // 2, 0, GRID_W - NA_KW)
    col_ok = (col[None, :] >= col_start[:, None]) & (col[None, :] < col_start[:, None] + NA_KW)
    dx_idx = np.clip(col[None, :] - col[:, None] + NA_KW - 1, 0, 2 * NA_KW - 2)
    deltas = sorted({r - int(np.clip(r - kh // 2, 0, rows - kh)) for r in range(rows)})
    tabs = []
    for delta in range(kh):
        dy_idx = np.clip(np.arange(kh) - delta + NA_KH_MAX - 1, 0, 2 * NA_KH_MAX - 2)
        bias = rpb[:, dy_idx[None, :, None], dx_idx[:, None, :]].astype(F32)
        bias = jnp.where(col_ok[None, :, None, :], bias, NEG_BIG)
        tabs.append(bias.reshape(B_HEADS * GRID_W, kh * GRID_W))
    del deltas
    return jnp.stack(tabs)


def _na_body(q_ref, k_ref, v_ref, bm_ref, gn_ref, pavg_ref, o_ref, *, rows, kh):
    r = pl.program_id(1)
    rs = jnp.clip(r - kh // 2, 0, rows - kh)
    start = pl.multiple_of(rs * GRID_W, GRID_W)
    q = q_ref[0]
    kw = k_ref[0, pl.ds(start, kh * GRID_W), :]
    vw = v_ref[0, pl.ds(start, kh * GRID_W), :]
    head = lax.broadcasted_iota(jnp.int32, (GRID_W, B_DIM), 1) // HEAD_DIM
    qs = jnp.concatenate([jnp.where(head == h, q, jnp.zeros_like(q)) for h in range(B_HEADS)], axis=0)
    s = _dot_nt(qs, kw) * (HEAD_DIM ** -0.5) + bm_ref[0]
    m = jnp.max(s, axis=-1, keepdims=True)
    p = jnp.exp(s - m)
    l = jnp.sum(p, axis=-1, keepdims=True)
    pv = _dot(p, vw) / l
    out = jnp.zeros((GRID_W, B_DIM), F32)
    for h in range(B_HEADS):
        out = out + jnp.where(head == h, pv[h * GRID_W:(h + 1) * GRID_W, :], 0.0)
    ms = _dot_lhs2(out * out, pavg_ref[...])
    o_ref[0] = (out * lax.rsqrt(ms + NORM_EPS) * gn_ref[...]).astype(o_ref.dtype)


def _natten(pb, rpb, out_norm, *, batch, seq_len):
    rows = seq_len // GRID_W
    kh = min(NA_KH_MAX, rows)
    tables = _na_tables(rpb, rows)
    pb3 = pb.reshape(batch, seq_len, B_COLS)

    def delta_map(bi, r):
        return (r - jnp.clip(r - kh // 2, 0, rows - kh), 0, 0)

    body = functools.partial(_na_body, rows=rows, kh=kh)
    out = pl.pallas_call(
        body, grid=(batch, rows),
        in_specs=[pl.BlockSpec((1, GRID_W, B_DIM), lambda bi, r: (bi, r, 0)),
                  pl.BlockSpec((1, seq_len, B_DIM), lambda bi, r: (bi, 0, 1)),
                  pl.BlockSpec((1, seq_len, B_DIM), lambda bi, r: (bi, 0, 2)),
                  pl.BlockSpec((1, B_HEADS * GRID_W, kh * GRID_W), delta_map),
                  pl.BlockSpec((1, B_DIM), lambda bi, r: (0, 0)),
                  pl.BlockSpec((B_DIM, B_DIM), lambda bi, r: (0, 0))],
        out_specs=pl.BlockSpec((1, GRID_W, B_DIM), lambda bi, r: (bi, r, 0)),
        out_shape=jax.ShapeDtypeStruct((batch, seq_len, B_DIM), BF16),
        compiler_params=_cparams(("parallel", "arbitrary")), name="natten")(
            pb3, pb3, pb3, tables, out_norm.reshape(1, B_DIM), _group_mat(B_DIM, HEAD_DIM, 1.0 / HEAD_DIM))
    return out.reshape(batch * seq_len, B_DIM)


def _hy_short_body(p_ref, pp_ref, pn_ref, cw_ref, cb_ref, v_o, x1_o, x2_o, *, blocks_per_seq):
    p = p_ref[...]
    prev, nxt = _shifted(p, pp_ref, pn_ref, blocks_per_seq)
    y = prev * cw_ref[0:1, :] + p * cw_ref[1:2, :] + nxt * cw_ref[2:3, :] + cb_ref[...]
    v_o[...] = y[:, 0:C_DIM]
    x1_o[...] = y[:, C_DIM:2 * C_DIM]
    x2_o[...] = y[:, 2 * C_DIM:]


def _hy_short(pc, lp, *, seq_len, tm=512):
    n = pc.shape[0]
    tok = lambda w: pl.BlockSpec((tm, w), lambda i: (i, 0))
    prev, nxt = _halo_specs(tm, C_COLS, n)
    cw = jnp.pad(lp['hy_conv_w'], ((0, SUBLANES - 3), (0, 0)))
    body = functools.partial(_hy_short_body, blocks_per_seq=seq_len // tm)
    return pl.pallas_call(
        body, grid=(n // tm,),
        in_specs=[tok(C_COLS), prev, nxt, _const_spec((SUBLANES, C_COLS)), _const_spec((1, C_COLS))],
        out_specs=[tok(C_DIM)] * 3,
        out_shape=[jax.ShapeDtypeStruct((n, C_DIM), F32)] * 3,
        compiler_params=_cparams(("parallel",)), name="hy_short")(
            pc, pc, pc, cw, lp['hy_conv_b'].reshape(1, C_COLS))


def _hy_filter_body(z_ref, t_ref, w1_ref, b1_ref, w2_ref, b2_ref, w3_ref, fr_ref, dl_ref, k_o, s_o):
    fr = fr_ref[...]
    h = jnp.sin(fr * (_dot3(z_ref[...], w1_ref[...]) + b1_ref[...]))
    h = jnp.sin(fr * (_dot3(h, w2_ref[...]) + b2_ref[...]))
    h = _dot3(h, w3_ref[...])
    tcol = t_ref[...]
    kern = h * jnp.exp(-tcol[:, 0:1] * jnp.abs(dl_ref[...])) * tcol[:, 1:2]
    k_o[...] = kern

    @pl.when(pl.program_id(0) == 0)
    def _():
        s_o[...] = jnp.zeros_like(s_o)

    s_o[...] += jnp.sum(jnp.abs(kern), axis=0, keepdims=True)


def _hy_filter(lp, seq_len, *, rb=512):
    ln = seq_len
    pos = np.concatenate([np.arange(ln), np.array([0]), np.arange(ln - 1, 0, -1)]).astype(np.float64)
    valid = np.ones(2 * ln)
    valid[ln] = 0.0
    t32 = np.linspace(0.0, 1.0, ln, dtype=np.float32)[:, None]
    w32 = (np.float32(2.0 * math.pi) * np.arange(ln, dtype=np.float32)[:, None] / np.float32(ln)).astype(np.float32)
    f32 = np.linspace(1e-4, HY_BANDS - 1, HY_BANDS, dtype=np.float32)[None, :]
    z32 = np.concatenate([t32, np.cos(f32 * w32), -np.sin(f32 * w32)], axis=-1).astype(np.float32)
    idx = pos.astype(np.int64)
    z = np.zeros((2 * ln, 64), np.float32)
    z[:, :HY_EMB] = z32[idx]
    tcol = np.stack([t32[idx, 0], valid.astype(np.float32)], axis=-1)
    max_decay = math.log(HY_TARGET) / HY_FAST_DECAY_PCT
    min_decay = math.log(HY_TARGET) / HY_SLOW_DECAY_PCT
    deltas = np.linspace(min_decay, max_decay, C_DIM, dtype=np.float32)
    dl = jnp.asarray(np.tile(deltas, HY_ORDER)[None, :])
    w1 = jnp.pad(lp['hy_ff_w1'], ((0, 64 - HY_EMB), (0, 0)))
    half = ln // rb
    return pl.pallas_call(
        _hy_filter_body, grid=(2 * ln // rb,),
        in_specs=[pl.BlockSpec((rb, 64), lambda i: (i, 0)), pl.BlockSpec((rb, 2), lambda i: (i, 0)),
                  _const_spec((64, HY_FF)), _const_spec((1, HY_FF)), _const_spec((HY_FF, HY_FF)),
                  _const_spec((1, HY_FF)),
                  pl.BlockSpec((HY_FF, HY_ORDER * C_DIM), lambda i: (0, i // half)),
                  _const_spec((1, HY_FF)), _const_spec((1, HY_ORDER * C_DIM))],
        out_specs=[pl.BlockSpec((rb, HY_ORDER * C_DIM), lambda i: (i, 0)),
                   pl.BlockSpec((1, HY_ORDER * C_DIM), lambda i: (0, 0))],
        out_shape=[jax.ShapeDtypeStruct((2 * ln, HY_ORDER * C_DIM), F32),
                   jax.ShapeDtypeStruct((1, HY_ORDER * C_DIM), F32)],
        compiler_params=_cparams(("arbitrary",)), name="hy_filter")(
            jnp.asarray(z), jnp.asarray(tcol), w1, lp['hy_ff_b1'].reshape(1, HY_FF), lp['hy_ff_w2'],
            lp['hy_ff_b2'].reshape(1, HY_FF), lp['hy_ff_w3'], lp['hy_sin_freq'].reshape(1, HY_FF), dl)


def _dft_split(n):
    n2 = 64 if n >= 8192 else 32
    return n // n2, n2


def _dft_tables(n):
    n1, n2 = _dft_split(n)
    k1 = np.arange(n1)[:, None]
    m1 = np.arange(n1)[None, :]
    ang1 = 2.0 * np.pi * (k1 * m1 % n1) / n1
    f1 = np.stack([np.cos(ang1), -np.sin(ang1)], axis=1).reshape(2 * n1, n1)
    f1inv = np.stack([np.cos(ang1), -np.sin(ang1)], axis=2).reshape(n1, 2 * n1) / n
    kk1 = np.arange(n1)[:, None, None]
    kk2 = np.arange(n2)[None, :, None]
    nn2 = np.arange(n2)[None, None, :]
    ang = 2.0 * np.pi * ((nn2 * (kk1 + n1 * kk2)) % n) / n
    c, s = np.cos(ang), np.sin(ang)
    g = np.concatenate([np.concatenate([c, s], axis=2), np.concatenate([-s, c], axis=2)], axis=1)
    ct, st = c.transpose(0, 2, 1), s.transpose(0, 2, 1)
    ginv = np.concatenate([np.concatenate([ct, -st], axis=2), np.concatenate([st, ct], axis=2)], axis=1)
    as_bf = lambda a: jnp.asarray(a.astype(np.float32), dtype=BF16)
    return as_bf(f1), as_bf(f1inv), as_bf(g), as_bf(ginv)


def _lmul_body(f_ref, x_ref, o_ref):
    o_ref[0] = jnp.dot(f_ref[...], x_ref[0].astype(BF16), preferred_element_type=F32).astype(o_ref.dtype)


def _lmul(f, x, out_dtype, *, lb=2048):
    bsz, k, lanes = x.shape
    m = f.shape[0]
    lb = min(lb, lanes)
    return pl.pallas_call(
        _lmul_body, grid=(bsz, lanes // lb),
        in_specs=[pl.BlockSpec((m, k), lambda bi, j: (0, 0)), pl.BlockSpec((1, k, lb), lambda bi, j: (bi, 0, j))],
        out_specs=pl.BlockSpec((1, m, lb), lambda bi, j: (bi, 0, j)),
        out_shape=jax.ShapeDtypeStruct((bsz, m, lanes), out_dtype),
        compiler_params=_cparams(("parallel", "parallel")), name="dft_outer")(f, x)


def _hy_spec_body(g_ref, a_ref, inv_ref, h_ref):
    kb = g_ref.shape[0]
    for i in range(kb):
        h_ref[i] = jnp.dot(g_ref[i], a_ref[i], preferred_element_type=F32) * inv_ref[...]


def _hy_spectrum(g, a, inv_norm, *, kb=8):
    n1, m, w = a.shape
    return pl.pallas_call(
        _hy_spec_body, grid=(n1 // kb,),
        in_specs=[pl.BlockSpec((kb, m, m), lambda i: (i, 0, 0)), pl.BlockSpec((kb, m, w), lambda i: (i, 0, 0)),
                  pl.BlockSpec((1, w), lambda i: (0, 0))],
        out_specs=pl.BlockSpec((kb, m, w), lambda i: (i, 0, 0)),
        out_shape=jax.ShapeDtypeStruct((n1, m, w), F32),
        compiler_params=_cparams(("parallel",)), name="hy_spectrum")(g, a, inv_norm)


def _hy_inner_body(g_ref, gi_ref, h_ref, a_ref, o_ref):
    kb = g_ref.shape[0]
    n2 = g_ref.shape[1] // 2
    for i in range(kb):
        x = jnp.dot(g_ref[i], a_ref[0, i], preferred_element_type=F32)
        xr, xi = x[0:n2], x[n2:]
        hr, hi = h_ref[i, 0:n2, :], h_ref[i, n2:, :]
        z = jnp.concatenate([xr * hr - xi * hi, xr * hi + xi * hr], axis=0).astype(BF16)
        o_ref[0, i] = jnp.dot(gi_ref[i], z, preferred_element_type=F32).astype(o_ref.dtype)


def _hy_inner(g, ginv, hspec, a, order, *, kb=8):
    bsz, n1, m, c = a.shape
    return pl.pallas_call(
        _hy_inner_body, grid=(n1 // kb, bsz),
        in_specs=[pl.BlockSpec((kb, m, m), lambda i, bi: (i, 0, 0)), pl.BlockSpec((kb, m, m), lambda i, bi: (i, 0, 0)),
                  pl.BlockSpec((kb, m, c), lambda i, bi: (i, 0, order)),
                  pl.BlockSpec((1, kb, m, c), lambda i, bi: (bi, i, 0, 0))],
        out_specs=pl.BlockSpec((1, kb, m, c), lambda i, bi: (bi, i, 0, 0)),
        out_shape=jax.ShapeDtypeStruct((bsz, n1, m, c), BF16),
        compiler_params=_cparams(("parallel", "parallel")), name="hy_inner")(g, ginv, hspec, a)


def _hy_gate_body(f_ref, a_ref, u_ref, x_ref, d_ref, o_ref):
    y = jnp.dot(f_ref[...], a_ref[0], preferred_element_type=F32)
    u = u_ref[0]
    o_ref[0] = x_ref[0] * (y + u * d_ref[...])


def _hy_gate(f1inv_half, a, u, xg, dskip, *, lb=2048):
    bsz, k, lanes = a.shape
    m = f1inv_half.shape[0]
    lb = min(lb, lanes)
    sig = pl.BlockSpec((1, m, lb), lambda bi, j: (bi, 0, j))
    return pl.pallas_call(
        _hy_gate_body, grid=(bsz, lanes // lb),
        in_specs=[pl.BlockSpec((m, k), lambda bi, j: (0, 0)), pl.BlockSpec((1, k, lb), lambda bi, j: (bi, 0, j)),
                  sig, sig, pl.BlockSpec((1, lb), lambda bi, j: (0, j))],
        out_specs=sig,
        out_shape=jax.ShapeDtypeStruct((bsz, m, lanes), F32),
        compiler_params=_cparams(("parallel", "parallel")), name="hy_gate")(f1inv_half, a, u, xg, dskip)


def _hyena(pc, lp, *, batch, seq_len):
    ln = seq_len
    n = 2 * ln
    n1, n2 = _dft_split(n)
    f1, f1inv, g, ginv = _dft_tables(n)
    v, x1, x2 = _hy_short(pc, lp, seq_len=ln)
    kern, l1 = _hy_filter(lp, ln)
    ka = _lmul(f1, kern.reshape(1, n1, n2 * HY_ORDER * C_DIM), BF16)
    hspec = _hy_spectrum(g, ka.reshape(n1, 2 * n2, HY_ORDER * C_DIM), 1.0 / l1)
    f1_half = f1[:, 0:n1 // 2]
    f1inv_half = f1inv[0:n1 // 2, :]
    view = lambda t: t.reshape(batch, n1 // 2, n2 * C_DIM)
    dsk = jnp.tile(lp['hy_bias'], (1, n2))

    def conv_gate(u, xg, order):
        a = _lmul(f1_half, u, BF16)
        a = _hy_inner(g, ginv, hspec, a.reshape(batch, n1, 2 * n2, C_DIM), order)
        return _hy_gate(f1inv_half, a.reshape(batch, 2 * n1, n2 * C_DIM), u, xg, dsk[order:order + 1])

    z = conv_gate(view(v), view(x1), 0)
    y = conv_gate(z, view(x2), 1)
    return y.reshape(batch * ln, C_DIM)


def _trunk(x, params, final_norm):
    batch, seq_len, _ = x.shape
    x = x.reshape(batch * seq_len, D_MODEL)
    mix = None
    for layer in range(DEPTH):
        lp = {name: arr[layer] for name, arr in params.items()}
        if mix is not None:
            prev = {name: arr[layer - 1] for name, arr in params.items()}
            x = _ffn(x, prev['norm_ffn2'], prev['ffn2_w_in'].astype(BF16), prev['ffn2_w_out'].astype(BF16), mix=mix)
        x = _ffn(x, lp['norm_ffn1'], lp['ffn1_w_in'].astype(BF16), lp['ffn1_w_out'].astype(BF16))
        w = lp['w_in']
        w_pad = jnp.concatenate([w[:, :A_COLS], jnp.zeros((D_MODEL, A_PAD - A_COLS), F32), w[:, A_COLS:]], axis=1)
        pa, pb, pc = _proj(x, lp['norm_mix'], w_pad.astype(BF16))
        prep = _rwkv_prep(pa, lp, seq_len=seq_len)
        yf, yb = _wkv(prep, lp, batch=batch, seq_len=seq_len)
        nb = _natten(pb, lp['na_rpb'], lp['na_out_norm'], batch=batch, seq_len=seq_len)
        yc = _hyena(pc, lp, batch=batch, seq_len=seq_len)
        mix = (yf, yb, prep[3], nb, yc, lp['w_out'].astype(BF16), lp['hy_out_norm'])
    last = {name: arr[DEPTH - 1] for name, arr in params.items()}
    x = _ffn(x, last['norm_ffn2'], last['ffn2_w_in'].astype(BF16), last['ffn2_w_out'].astype(BF16), mix=mix,
             final_g=final_norm)
    return x.reshape(batch, seq_len, D_MODEL)


def kernel(x_prompt, x_sample, norm_ffn1, ffn1_w_in, ffn1_w_out, norm_mix, w_in, rwkv_mu, rwkv_w0, rwkv_w_up, rwkv_a0, rwkv_a_up, rwkv_g_up, rwkv_k_k, rwkv_k_a, rwkv_r_k, rwkv_ln_g, rwkv_ln_b, na_rpb, na_out_norm, hy_conv_w, hy_conv_b, hy_ff_w1, hy_ff_b1, hy_ff_w2, hy_ff_b2, hy_ff_w3, hy_sin_freq, hy_bias, hy_out_norm, w_out, norm_ffn2, ffn2_w_in, ffn2_w_out, final_norm):
    params = {
        'norm_ffn1': norm_ffn1, 'ffn1_w_in': ffn1_w_in, 'ffn1_w_out': ffn1_w_out,
        'norm_mix': norm_mix, 'w_in': w_in,
        'rwkv_mu': rwkv_mu, 'rwkv_w0': rwkv_w0, 'rwkv_w_up': rwkv_w_up, 'rwkv_a0': rwkv_a0,
        'rwkv_a_up': rwkv_a_up, 'rwkv_g_up': rwkv_g_up, 'rwkv_k_k': rwkv_k_k, 'rwkv_k_a': rwkv_k_a,
        'rwkv_r_k': rwkv_r_k.reshape(DEPTH, A_DIM), 'rwkv_ln_g': rwkv_ln_g, 'rwkv_ln_b': rwkv_ln_b,
        'na_rpb': na_rpb, 'na_out_norm': na_out_norm,
        'hy_conv_w': hy_conv_w, 'hy_conv_b': hy_conv_b, 'hy_ff_w1': hy_ff_w1, 'hy_ff_b1': hy_ff_b1,
        'hy_ff_w2': hy_ff_w2, 'hy_ff_b2': hy_ff_b2, 'hy_ff_w3': hy_ff_w3, 'hy_sin_freq': hy_sin_freq,
        'hy_bias': hy_bias, 'hy_out_norm': hy_out_norm,
        'w_out': w_out, 'norm_ffn2': norm_ffn2, 'ffn2_w_in': ffn2_w_in, 'ffn2_w_out': ffn2_w_out,
    }
    return (_trunk(x_prompt, params, final_norm), _trunk(x_sample, params, final_norm))
```

```python
import functools
import math

import numpy as np
import jax
import jax.numpy as jnp
from jax import lax
from jax.experimental import pallas as pl
from jax.experimental.pallas import tpu as pltpu

F32 = jnp.float32
BF16 = jnp.bfloat16

D_MODEL = 1024
DEPTH = 4
GRID_W = 64
HEAD_DIM = 64
A_HEADS = 4
A_DIM = A_HEADS * HEAD_DIM
A_LORA_W = 32
A_LORA_A = 32
A_LORA_G = 64
B_HEADS = 8
B_DIM = B_HEADS * HEAD_DIM
NA_KH_MAX = 8
NA_KW = 16
C_DIM = D_MODEL - A_DIM - B_DIM
C_GROUPS = C_DIM // HEAD_DIM
HY_ORDER = 2
HY_BANDS = 16
HY_EMB = 1 + 2 * HY_BANDS
HY_FF = 64
HY_FAST_DECAY_PCT = 0.3
HY_SLOW_DECAY_PCT = 1.5
HY_TARGET = 1e-2
D_FF = 2816
NORM_EPS = 1e-5
GN_EPS = 64e-5
A_COLS = 3 * A_DIM + 2 * A_LORA_W + 2 * A_LORA_A + A_LORA_G
A_PAD = 1024
B_COLS = 3 * B_DIM
C_COLS = (HY_ORDER + 1) * C_DIM

SUBLANES = 8
VMEM_LIMIT = 56 * 1024 * 1024
WKV_CHUNK = 64
NEG_BIG = -1e30
DFT_ROWS = SUBLANES


def _cparams(sem):
    return pltpu.CompilerParams(dimension_semantics=sem, vmem_limit_bytes=VMEM_LIMIT)


def _const_spec(shape):
    nd = len(shape)
    return pl.BlockSpec(shape, lambda *_: (0,) * nd, pipeline_mode=pl.Buffered(1))


def _dot(a, b):
    return jnp.dot(a.astype(BF16), b.astype(BF16), preferred_element_type=F32)


def _dot_nt(a, b):
    return lax.dot_general(a.astype(BF16), b.astype(BF16), (((1,), (1,)), ((), ())),
                           preferred_element_type=F32)


def _dot_tn(a, b):
    return lax.dot_general(a.astype(BF16), b.astype(BF16), (((0,), (0,)), ((), ())),
                           preferred_element_type=F32)


def _split(a):
    hi = a.astype(BF16)
    lo = (a - hi.astype(F32)).astype(BF16)
    return hi, lo


def _dot_lhs2(a, b_bf16):
    hi, lo = _split(a)
    return (jnp.dot(hi, b_bf16, preferred_element_type=F32)
            + jnp.dot(lo, b_bf16, preferred_element_type=F32))


def _dot3(a, b):
    ah, al = _split(a)
    bh, bl = _split(b)
    return (jnp.dot(ah, bh, preferred_element_type=F32)
            + jnp.dot(al, bh, preferred_element_type=F32)
            + jnp.dot(ah, bl, preferred_element_type=F32))


def _rms(x, g):
    return x * lax.rsqrt(jnp.mean(x * x, axis=-1, keepdims=True) + NORM_EPS) * g


def _group_mat(n, group, value):
    idx = np.arange(n) // group
    return jnp.asarray((idx[:, None] == idx[None, :]).astype(np.float32) * value, dtype=BF16)


def _ffn_body(*refs, n_chunks, with_mix, with_final):
    it = iter(refs)
    x_ref = next(it)
    if with_mix:
        yf_ref, yb_ref, g_ref, nb_ref, yc_ref, wo_ref, cn_ref, pavg_ref = (next(it) for _ in range(8))
    ng_ref, win_ref, wout_ref = next(it), next(it), next(it)
    if with_final:
        fn_ref = next(it)
    o_ref = next(it)

    x = x_ref[...]
    if with_mix:
        ya = (yf_ref[...] + yb_ref[...]) * g_ref[...]
        yc = yc_ref[...]
        ms = _dot_lhs2(yc * yc, pavg_ref[...])
        yc = yc * lax.rsqrt(ms + NORM_EPS) * cn_ref[...]
        x = (x + _dot(ya, wo_ref[0:A_DIM, :]) + _dot(nb_ref[...], wo_ref[A_DIM:A_DIM + B_DIM, :])
             + _dot(yc, wo_ref[A_DIM + B_DIM:, :]))
    h = _rms(x, ng_ref[...]).astype(BF16)
    cw = D_FF // n_chunks
    acc = jnp.zeros_like(x)
    for c in range(n_chunks):
        gate = jnp.dot(h, win_ref[:, c * cw:(c + 1) * cw], preferred_element_type=F32)
        up = jnp.dot(h, win_ref[:, D_FF + c * cw:D_FF + (c + 1) * cw], preferred_element_type=F32)
        act = (gate * jax.nn.sigmoid(gate) * up).astype(BF16)
        acc = acc + jnp.dot(act, wout_ref[c * cw:(c + 1) * cw, :], preferred_element_type=F32)
    y = x + 0.5 * acc
    if with_final:
        y = _rms(y, fn_ref[...])
    o_ref[...] = y


def _ffn(x, norm_g, w_in, w_out, *, mix=None, final_g=None, tm=512, n_chunks=2):
    n = x.shape[0]
    tok = lambda w: pl.BlockSpec((tm, w), lambda i: (i, 0))
    args, specs = [x], [tok(D_MODEL)]
    if mix is not None:
        yf, yb, g, nb, yc, wo, cn = mix
        args += [yf, yb, g, nb, yc, wo, cn.reshape(1, C_DIM), _group_mat(C_DIM, HEAD_DIM, 1.0 / HEAD_DIM)]
        specs += [tok(A_DIM), tok(A_DIM), tok(A_DIM), tok(B_DIM), tok(C_DIM),
                  _const_spec((D_MODEL, D_MODEL)), _const_spec((1, C_DIM)), _const_spec((C_DIM, C_DIM))]
    args += [norm_g.reshape(1, D_MODEL), w_in, w_out]
    specs += [_const_spec((1, D_MODEL)), _const_spec((D_MODEL, 2 * D_FF)), _const_spec((D_FF, D_MODEL))]
    if final_g is not None:
        args.append(final_g.reshape(1, D_MODEL))
        specs.append(_const_spec((1, D_MODEL)))
    body = functools.partial(_ffn_body, n_chunks=n_chunks, with_mix=mix is not None,
                             with_final=final_g is not None)
    return pl.pallas_call(
        body, grid=(n // tm,), in_specs=specs, out_specs=tok(D_MODEL),
        out_shape=jax.ShapeDtypeStruct((n, D_MODEL), F32),
        compiler_params=_cparams(("parallel",)), name="ffn")(*args)


def _proj_body(x_ref, g_ref, w_ref, pa_ref, pb_ref, pc_ref):
    h = _rms(x_ref[...], g_ref[...]).astype(BF16)
    pa_ref[...] = jnp.dot(h, w_ref[:, 0:A_PAD], preferred_element_type=F32)
    pb_ref[...] = jnp.dot(h, w_ref[:, A_PAD:A_PAD + B_COLS], preferred_element_type=F32).astype(BF16)
    pc_ref[...] = jnp.dot(h, w_ref[:, A_PAD + B_COLS:], preferred_element_type=F32)


def _proj(x, norm_g, w_pad, *, tm=512):
    n = x.shape[0]
    tok = lambda w: pl.BlockSpec((tm, w), lambda i: (i, 0))
    wcols = A_PAD + B_COLS + C_COLS
    return pl.pallas_call(
        _proj_body, grid=(n // tm,),
        in_specs=[tok(D_MODEL), _const_spec((1, D_MODEL)), _const_spec((D_MODEL, wcols))],
        out_specs=[tok(A_PAD), tok(B_COLS), tok(C_COLS)],
        out_shape=[jax.ShapeDtypeStruct((n, A_PAD), F32), jax.ShapeDtypeStruct((n, B_COLS), BF16),
                   jax.ShapeDtypeStruct((n, C_COLS), F32)],
        compiler_params=_cparams(("parallel",)), name="proj")(x, norm_g.reshape(1, D_MODEL), w_pad)


def _halo_specs(tm, width, n_tok):
    per = tm // SUBLANES
    last = n_tok // SUBLANES - 1
    prev = pl.BlockSpec((SUBLANES, width), lambda i: (jnp.maximum(i * per - 1, 0), 0))
    nxt = pl.BlockSpec((SUBLANES, width), lambda i: (jnp.minimum((i + 1) * per, last), 0))
    return prev, nxt


def _shifted(p, prev_ref, next_ref, blocks_per_seq):
    tm = p.shape[0]
    j = pl.program_id(0) % blocks_per_seq
    hp = jnp.where(j == 0, 0.0, prev_ref[SUBLANES - 1:SUBLANES, :])
    hn = jnp.where(j == blocks_per_seq - 1, 0.0, next_ref[0:1, :])
    row = lax.broadcasted_iota(jnp.int32, p.shape, 0)
    prev = jnp.where(row == 0, hp, pltpu.roll(p, 1, axis=0))
    nxt = jnp.where(row == tm - 1, hn, pltpu.roll(p, tm - 1, axis=0))
    return prev, nxt


def _rwkv_prep_body(p_ref, pp_ref, pn_ref, mu_ref, wlo_ref, w0a0_ref, gup_ref, kk_ref, ka_ref, rk_ref,
                    hsum_ref, r_o, v_o, kk_o, g_o, lw0_o, lw1_o, kd0_o, kd1_o, b0_o, b1_o, bv0_o, bv1_o,
                    *, blocks_per_seq):
    p = p_ref[...]
    prev, nxt = _shifted(p, pp_ref, pn_ref, blocks_per_seq)
    p = p + mu_ref[...] * (0.5 * (prev + nxt) - p)
    r = p[:, 0:A_DIM]
    k = p[:, A_DIM:2 * A_DIM]
    v = p[:, 2 * A_DIM:3 * A_DIM]
    lo = p[:, 3 * A_DIM:3 * A_DIM + 128]
    gd = p[:, 3 * A_DIM + 128:3 * A_DIM + 256]
    lane = lax.broadcasted_iota(jnp.int32, lo.shape, 1)
    lo = jnp.where(lane < 2 * A_LORA_W, jnp.tanh(lo), lo)
    wa = _dot3(lo, wlo_ref[...]) + w0a0_ref[...]
    g = _dot3(jax.nn.sigmoid(gd), gup_ref[...])
    hsum = hsum_ref[...]
    kkv = k * kk_ref[...]
    nrm = jnp.sqrt(_dot_lhs2(kkv * kkv, hsum))
    kkv = kkv / jnp.maximum(nrm, 1e-12)
    r_o[...] = r
    v_o[...] = v
    kk_o[...] = kkv
    g_o[...] = g
    for d, (lw_o, kd_o, b_o, bv_o) in enumerate(((lw0_o, kd0_o, b0_o, bv0_o), (lw1_o, kd1_o, b1_o, bv1_o))):
        w_log = -jax.nn.softplus(-wa[:, d * A_DIM:(d + 1) * A_DIM]) - 0.5
        a = jax.nn.sigmoid(wa[:, (2 + d) * A_DIM:(3 + d) * A_DIM])
        kd = k * (1.0 + (a - 1.0) * ka_ref[...])
        lw_o[...] = -jnp.exp(w_log)
        kd_o[...] = kd
        b_o[...] = kkv * a
        bv_o[...] = _dot_lhs2(r * kd * rk_ref[...], hsum) * v


def _rwkv_prep(pa, lp, *, seq_len, tm=512):
    n = pa.shape[0]
    tok = lambda w: pl.BlockSpec((tm, w), lambda i: (i, 0))
    prev, nxt = _halo_specs(tm, A_PAD, n)
    mu = jnp.pad(lp['rwkv_mu'], (0, A_PAD - A_COLS)).reshape(1, A_PAD)
    wlo = jnp.zeros((128, 4 * A_DIM), F32)
    for d in range(2):
        wlo = wlo.at[d * A_LORA_W:(d + 1) * A_LORA_W, d * A_DIM:(d + 1) * A_DIM].set(lp['rwkv_w_up'][d])
        wlo = wlo.at[64 + d * A_LORA_A:64 + (d + 1) * A_LORA_A, (2 + d) * A_DIM:(3 + d) * A_DIM].set(lp['rwkv_a_up'][d])
    w0a0 = jnp.concatenate([lp['rwkv_w0'].reshape(-1), lp['rwkv_a0'].reshape(-1)]).reshape(1, 4 * A_DIM)
    gup = jnp.pad(lp['rwkv_g_up'], ((0, 128 - A_LORA_G), (0, 0)))
    consts = [mu, wlo, w0a0, gup, lp['rwkv_k_k'].reshape(1, A_DIM), lp['rwkv_k_a'].reshape(1, A_DIM),
              lp['rwkv_r_k'].reshape(1, A_DIM), _group_mat(A_DIM, HEAD_DIM, 1.0)]
    body = functools.partial(_rwkv_prep_body, blocks_per_seq=seq_len // tm)
    return pl.pallas_call(
        body, grid=(n // tm,),
        in_specs=[tok(A_PAD), prev, nxt] + [_const_spec(c.shape) for c in consts],
        out_specs=[tok(A_DIM)] * 12,
        out_shape=[jax.ShapeDtypeStruct((n, A_DIM), F32)] * 12,
        compiler_params=_cparams(("parallel",)), name="rwkv_prep")(pa, pa, pa, *consts)


def _wkv_setup(r, v, kk, lw, kd, b, reverse):
    tb = r.shape[0]
    c_len = WKV_CHUNK
    pos = lax.broadcasted_iota(jnp.int32, (tb, A_DIM), 0) % c_len
    cum = lw
    sh = 1
    while sh < c_len:
        if reverse:
            cum = cum + jnp.where(pos < c_len - sh, pltpu.roll(cum, tb - sh, axis=0), 0.0)
        else:
            cum = cum + jnp.where(pos >= sh, pltpu.roll(cum, sh, axis=0), 0.0)
        sh *= 2
    e_neg = jnp.exp(-cum)
    return dict(cum=cum, rq=r * jnp.exp(cum), kq=kk * jnp.exp(cum - lw), kbar=kd * e_neg, bbar=b * e_neg, v=v)


def _wkv_body(rf, vf, kkf, lwf, kdf, bf, bvf, rb, vb, kkb, lwb, kdb, bb, bvb, lng_ref, lnb_ref, pavg_ref,
              of_ref, ob_ref, sf_ref, sb_ref):
    @pl.when(pl.program_id(1) == 0)
    def _():
        sf_ref[...] = jnp.zeros_like(sf_ref)
        sb_ref[...] = jnp.zeros_like(sb_ref)

    c_len = WKV_CHUNK
    tb = rf.shape[0]
    nc = tb // c_len
    dirs = (_wkv_setup(rf[...], vf[...], kkf[...], lwf[...], kdf[...], bf[...], False),
            _wkv_setup(rb[...], vb[...], kkb[...], lwb[...], kdb[...], bb[...], True))
    s_refs, o_refs, bvs = (sf_ref, sb_ref), (of_ref, ob_ref), (bvf, bvb)

    t_idx = lax.broadcasted_iota(jnp.int32, (c_len, A_DIM), 0)
    s_idx = lax.broadcasted_iota(jnp.int32, (c_len, A_DIM), 1) % c_len
    incl = (s_idx <= t_idx, s_idx >= t_idx)
    strict = (s_idx < t_idx, s_idx > t_idx)
    eye = (s_idx == t_idx).astype(F32)
    bd_mask = (lax.broadcasted_iota(jnp.int32, (A_DIM, A_DIM), 0) // HEAD_DIM
               == lax.broadcasted_iota(jnp.int32, (A_DIM, A_DIM), 1) // HEAD_DIM)
    zero_bf = jnp.zeros((A_DIM, A_DIM), BF16)

    def stack(x):
        hi, lo = _split(x)
        return (jnp.where(bd_mask, jnp.concatenate([hi] * A_HEADS, axis=0), zero_bf),
                jnp.where(bd_mask, jnp.concatenate([lo] * A_HEADS, axis=0), zero_bf))

    def mm(a, b_hl, nt=False):
        bh, bl = b_hl
        ah, al = _split(a)
        m = a.shape[0]
        dg = _dot_nt if nt else _dot
        p = dg(jnp.concatenate([ah, al], axis=0), bh)
        return p[0:m] + p[m:] + dg(ah, bl)

    items = [(d, c) for c in range(nc) for d in (0, 1)]
    ch = {}
    for d, c in items:
        sl = slice(c * c_len, (c + 1) * c_len)
        q = dirs[d]
        ch[d, c] = dict(rq=q['rq'][sl], kq=q['kq'][sl], kbar=q['kbar'][sl], bbar=q['bbar'][sl], v=q['v'][sl])
    for key in items:
        e = ch[key]
        kst, bst = stack(e['kbar']), stack(e['bbar'])
        acat = mm(jnp.concatenate([e['rq'], e['kq']], axis=0),
                  (jnp.concatenate([kst[0], bst[0]], axis=0), jnp.concatenate([kst[1], bst[1]], axis=0)),
                  nt=True)
        d = key[0]
        e['a_rv'] = jnp.concatenate([jnp.where(incl[d], acat[0:c_len, 0:A_DIM], 0.0),
                                     jnp.where(strict[d], acat[c_len:, 0:A_DIM], 0.0)], axis=0)
        e['a_rb'] = jnp.where(incl[d], acat[0:c_len, A_DIM:], 0.0)
        e['a_kb'] = jnp.where(strict[d], acat[c_len:, A_DIM:], 0.0)
    half = 1
    while half < c_len:
        in_block = (t_idx // (2 * half)) == (s_idx // (2 * half))
        lo_t, lo_s = t_idx % (2 * half) < half, s_idx % (2 * half) < half
        off = (in_block & ~lo_t & lo_s, in_block & lo_t & ~lo_s)
        for key in items:
            e = ch[key]
            e['a_off'] = jnp.where(off[key[0]], e['a_kb'], 0.0)
        if half == 1:
            for key in items:
                ch[key]['tinv'] = eye - ch[key]['a_off']
        else:
            for key in items:
                ch[key]['tmp'] = mm(ch[key]['tinv'], stack(ch[key]['a_off']))
            for key in items:
                ch[key]['tinv'] = ch[key]['tinv'] - mm(ch[key]['tmp'], stack(ch[key]['tinv']))
        half *= 2
    for key in items:
        e = ch[key]
        av = mm(e['a_rv'], stack(e['v']))
        e['o_intra'], e['akkv'] = av[0:c_len], av[c_len:]
    for key in items:
        e = ch[key]
        e['w_t'] = mm(e['tinv'], stack(e['kq']))
        e['u0'] = mm(e['tinv'], stack(e['akkv']))
    outs = ([None] * nc, [None] * nc)
    for step in range(nc):
        cur = [(0, step), (1, nc - 1 - step)]
        xs = {}
        for d, c in cur:
            e = ch[d, c]
            e['s_old'] = s_refs[d][...]
            xs[d] = mm(jnp.concatenate([e['rq'], e['w_t']], axis=0), _split(e['s_old']), nt=True)
        for d, c in cur:
            e = ch[d, c]
            u = -(xs[d][c_len:] + e['u0'])
            edge = c * c_len if d else (c + 1) * c_len - 1
            e_end = jnp.exp(dirs[d]['cum'][edge:edge + 1, :])
            vu_h, vu_l = _split(jnp.concatenate([e['v'], u], axis=0))
            kb_h, kb_l = _split(jnp.concatenate([e['kbar'] * e_end, e['bbar'] * e_end], axis=0))
            ds = _dot_tn(jnp.concatenate([vu_h, vu_l, vu_h], axis=0), jnp.concatenate([kb_h, kb_h, kb_l], axis=0))
            s_refs[d][...] = e['s_old'] * e_end + jnp.where(bd_mask, ds, 0.0)
            outs[d][c] = xs[d][0:c_len] + e['o_intra'] + mm(e['a_rb'], stack(u))
    lng, lnb, pavg = lng_ref[...], lnb_ref[...], pavg_ref[...]
    for d in (0, 1):
        o = jnp.concatenate(outs[d], axis=0)
        mu = _dot_lhs2(o, pavg)
        dev = o - mu
        var = _dot_lhs2(dev * dev, pavg)
        o_refs[d][...] = dev * lax.rsqrt(var + GN_EPS) * lng + lnb + bvs[d][...]


def _wkv(prep, lp, *, batch, seq_len, tb=256):
    r, v, kk, _, lw0, lw1, kd0, kd1, b0, b1, bv0, bv1 = prep
    n = r.shape[0]
    nblk = seq_len // tb
    fwd = pl.BlockSpec((tb, A_DIM), lambda bi, j: (bi * nblk + j, 0))
    bwd = pl.BlockSpec((tb, A_DIM), lambda bi, j: (bi * nblk + nblk - 1 - j, 0))
    consts = [lp['rwkv_ln_g'].reshape(1, A_DIM), lp['rwkv_ln_b'].reshape(1, A_DIM),
              _group_mat(A_DIM, HEAD_DIM, 1.0 / HEAD_DIM)]
    cspec = lambda shape: pl.BlockSpec(shape, lambda bi, j: (0, 0), pipeline_mode=pl.Buffered(1))
    return pl.pallas_call(
        _wkv_body, grid=(batch, nblk),
        in_specs=[fwd] * 7 + [bwd] * 7 + [cspec(c.shape) for c in consts],
        out_specs=[fwd, bwd],
        out_shape=[jax.ShapeDtypeStruct((n, A_DIM), F32)] * 2,
        scratch_shapes=[pltpu.VMEM((A_DIM, A_DIM), F32), pltpu.VMEM((A_DIM, A_DIM), F32)],
        compiler_params=_cparams(("parallel", "arbitrary")), name="wkv")(
            r, v, kk, lw0, kd0, b0, bv0, r, v, kk, lw1, kd1, b1, bv1, *consts)


def _na_tables(rpb, kh):
    col = np.arange(GRID_W)
    col_start = np.clip(col - NA_KW // 2, 0, GRID_W - NA_KW)
    col_ok = (col[None, :] >= col_start[:, None]) & (col[None, :] < col_start[:, None] + NA_KW)
    dx_idx = np.clip(col[None, :] - col[:, None] + NA_KW - 1, 0, 2 * NA_KW - 2)
    onehot = ((np.arange(2 * NA_KW - 1)[:, None, None] == dx_idx[None]) & col_ok[None]).astype(np.float32)
    mask_add = np.where(col_ok, 0.0, NEG_BIG).astype(np.float32)
    rpbx = jnp.einsum('hdx,xqk->hdqk', rpb.astype(F32), jnp.asarray(onehot),
                      precision=lax.Precision.HIGHEST) + jnp.asarray(mask_add)
    tabs = []
    for delta in range(kh):
        lo = NA_KH_MAX - 1 - delta
        tab = jnp.transpose(rpbx[:, lo:lo + kh], (0, 2, 1, 3))
        tabs.append(tab.reshape(B_HEADS * GRID_W, kh * GRID_W))
    return jnp.stack(tabs)


def _na_body(*refs, rows, kh, rq):
    q_ref, k_ref, v_ref = refs[0:3]
    bm_refs = refs[3:3 + rq]
    gn_ref, pavg_ref, o_ref = refs[3 + rq:]
    head = lax.broadcasted_iota(jnp.int32, (GRID_W, B_DIM), 1) // HEAD_DIM
    wins, ss = [], []
    for i in range(rq):
        r = pl.program_id(1) * rq + i
        rs = jnp.clip(r - kh // 2, 0, rows - kh)
        start = pl.multiple_of(rs * GRID_W, GRID_W)
        wins.append(start)
        q = q_ref[0, i * GRID_W:(i + 1) * GRID_W, :]
        kw = k_ref[0, pl.ds(start, kh * GRID_W), :]
        qs = jnp.concatenate([jnp.where(head == h, q, jnp.zeros_like(q)) for h in range(B_HEADS)], axis=0)
        ss.append(_dot_nt(qs, kw))
    ps, ls = [], []
    for i in range(rq):
        s = ss[i] * (HEAD_DIM ** -0.5) + bm_refs[i][0]
        m = jnp.max(s, axis=-1, keepdims=True)
        p = jnp.exp(s - m)
        ls.append(jnp.sum(p, axis=-1, keepdims=True))
        ps.append(p.astype(BF16))
    pvs = [_dot(ps[i], v_ref[0, pl.ds(wins[i], kh * GRID_W), :]) / ls[i] for i in range(rq)]
    outs = []
    for i in range(rq):
        out = jnp.zeros((GRID_W, B_DIM), F32)
        for h in range(B_HEADS):
            out = out + jnp.where(head == h, pvs[i][h * GRID_W:(h + 1) * GRID_W, :], 0.0)
        outs.append(out)
    out = jnp.concatenate(outs, axis=0)
    ms = _dot_lhs2(out * out, pavg_ref[...])
    o_ref[0] = (out * lax.rsqrt(ms + NORM_EPS) * gn_ref[...]).astype(o_ref.dtype)


def _natten(pb, rpb, out_norm, *, batch, seq_len, rq=4):
    rows = seq_len // GRID_W
    kh = min(NA_KH_MAX, rows)
    tables = _na_tables(rpb, kh)
    pb3 = pb.reshape(batch, seq_len, B_COLS)

    def delta_map(i):
        def index(bi, j):
            r = j * rq + i
            return (r - jnp.clip(r - kh // 2, 0, rows - kh), 0, 0)
        return index

    body = functools.partial(_na_body, rows=rows, kh=kh, rq=rq)
    out = pl.pallas_call(
        body, grid=(batch, rows // rq),
        in_specs=[pl.BlockSpec((1, rq * GRID_W, B_DIM), lambda bi, j: (bi, j, 0)),
                  pl.BlockSpec((1, seq_len, B_DIM), lambda bi, j: (bi, 0, 1)),
                  pl.BlockSpec((1, seq_len, B_DIM), lambda bi, j: (bi, 0, 2))]
                 + [pl.BlockSpec((1, B_HEADS * GRID_W, kh * GRID_W), delta_map(i)) for i in range(rq)]
                 + [pl.BlockSpec((1, B_DIM), lambda bi, j: (0, 0)),
                    pl.BlockSpec((B_DIM, B_DIM), lambda bi, j: (0, 0))],
        out_specs=pl.BlockSpec((1, rq * GRID_W, B_DIM), lambda bi, j: (bi, j, 0)),
        out_shape=jax.ShapeDtypeStruct((batch, seq_len, B_DIM), BF16),
        compiler_params=_cparams(("parallel", "arbitrary")), name="natten")(
            pb3, pb3, pb3, *([tables] * rq), out_norm.reshape(1, B_DIM),
            _group_mat(B_DIM, HEAD_DIM, 1.0 / HEAD_DIM))
    return out.reshape(batch * seq_len, B_DIM)


def _hy_short_body(p_ref, pp_ref, pn_ref, cw_ref, cb_ref, v_o, x1_o, x2_o, *, blocks_per_seq):
    p = p_ref[...]
    prev, nxt = _shifted(p, pp_ref, pn_ref, blocks_per_seq)
    y = prev * cw_ref[0:1, :] + p * cw_ref[1:2, :] + nxt * cw_ref[2:3, :] + cb_ref[...]
    v_o[...] = y[:, 0:C_DIM]
    x1_o[...] = y[:, C_DIM:2 * C_DIM]
    x2_o[...] = y[:, 2 * C_DIM:]


def _hy_short(pc, lp, *, seq_len, tm=512):
    n = pc.shape[0]
    tok = lambda w: pl.BlockSpec((tm, w), lambda i: (i, 0))
    prev, nxt = _halo_specs(tm, C_COLS, n)
    cw = jnp.pad(lp['hy_conv_w'], ((0, SUBLANES - 3), (0, 0)))
    body = functools.partial(_hy_short_body, blocks_per_seq=seq_len // tm)
    return pl.pallas_call(
        body, grid=(n // tm,),
        in_specs=[tok(C_COLS), prev, nxt, _const_spec((SUBLANES, C_COLS)), _const_spec((1, C_COLS))],
        out_specs=[tok(C_DIM)] * 3,
        out_shape=[jax.ShapeDtypeStruct((n, C_DIM), F32)] * 3,
        compiler_params=_cparams(("parallel",)), name="hy_short")(
            pc, pc, pc, cw, lp['hy_conv_b'].reshape(1, C_COLS))


def _hy_filter_body(z_ref, t_ref, w1_ref, b1_ref, w2_ref, b2_ref, w3_ref, fr_ref, dl_ref, k_o, s_o):
    fr = fr_ref[...]
    h = jnp.sin(fr * (_dot3(z_ref[...], w1_ref[...]) + b1_ref[...]))
    h = jnp.sin(fr * (_dot3(h, w2_ref[...]) + b2_ref[...]))
    h = _dot3(h, w3_ref[...])
    tcol = t_ref[...]
    kern = h * jnp.exp(-tcol[:, 0:1] * jnp.abs(dl_ref[...])) * tcol[:, 1:2]
    k_o[...] = kern

    @pl.when(pl.program_id(0) == 0)
    def _():
        s_o[...] = jnp.zeros_like(s_o)

    s_o[...] += jnp.sum(jnp.abs(kern), axis=0, keepdims=True)


def _hy_filter(lp, seq_len, *, rb=512):
    ln = seq_len
    pos = np.concatenate([np.arange(ln), np.array([0]), np.arange(ln - 1, 0, -1)]).astype(np.float64)
    valid = np.ones(2 * ln)
    valid[ln] = 0.0
    t32 = np.linspace(0.0, 1.0, ln, dtype=np.float32)[:, None]
    w32 = (np.float32(2.0 * math.pi) * np.arange(ln, dtype=np.float32)[:, None] / np.float32(ln)).astype(np.float32)
    f32 = np.linspace(1e-4, HY_BANDS - 1, HY_BANDS, dtype=np.float32)[None, :]
    z32 = np.concatenate([t32, np.cos(f32 * w32), -np.sin(f32 * w32)], axis=-1).astype(np.float32)
    idx = pos.astype(np.int64)
    z = np.zeros((2 * ln, 64), np.float32)
    z[:, :HY_EMB] = z32[idx]
    tcol = np.stack([t32[idx, 0], valid.astype(np.float32)], axis=-1)
    max_decay = math.log(HY_TARGET) / HY_FAST_DECAY_PCT
    min_decay = math.log(HY_TARGET) / HY_SLOW_DECAY_PCT
    deltas = np.linspace(min_decay, max_decay, C_DIM, dtype=np.float32)
    dl = jnp.asarray(np.tile(deltas, HY_ORDER)[None, :])
    w1 = jnp.pad(lp['hy_ff_w1'], ((0, 64 - HY_EMB), (0, 0)))
    half = ln // rb
    return pl.pallas_call(
        _hy_filter_body, grid=(2 * ln // rb,),
        in_specs=[pl.BlockSpec((rb, 64), lambda i: (i, 0)), pl.BlockSpec((rb, 2), lambda i: (i, 0)),
                  _const_spec((64, HY_FF)), _const_spec((1, HY_FF)), _const_spec((HY_FF, HY_FF)),
                  _const_spec((1, HY_FF)),
                  pl.BlockSpec((HY_FF, HY_ORDER * C_DIM), lambda i: (0, i // half)),
                  _const_spec((1, HY_FF)), _const_spec((1, HY_ORDER * C_DIM))],
        out_specs=[pl.BlockSpec((rb, HY_ORDER * C_DIM), lambda i: (i, 0)),
                   pl.BlockSpec((1, HY_ORDER * C_DIM), lambda i: (0, 0))],
        out_shape=[jax.ShapeDtypeStruct((2 * ln, HY_ORDER * C_DIM), F32),
                   jax.ShapeDtypeStruct((1, HY_ORDER * C_DIM), F32)],
        compiler_params=_cparams(("arbitrary",)), name="hy_filter")(
            jnp.asarray(z), jnp.asarray(tcol), w1, lp['hy_ff_b1'].reshape(1, HY_FF), lp['hy_ff_w2'],
            lp['hy_ff_b2'].reshape(1, HY_FF), lp['hy_ff_w3'], lp['hy_sin_freq'].reshape(1, HY_FF), dl)


def _dft_split(n):
    n2 = 64 if n >= 8192 else 32
    return n // n2, n2


def _dft_tables(n):
    n1, n2 = _dft_split(n)
    k1 = np.arange(n1)[:, None]
    m1 = np.arange(n1)[None, :]
    ang1 = 2.0 * np.pi * (k1 * m1 % n1) / n1
    f1 = np.stack([np.cos(ang1), -np.sin(ang1)], axis=1).reshape(2 * n1, n1)
    f1inv = np.stack([np.cos(ang1), -np.sin(ang1)], axis=2).reshape(n1, 2 * n1) / n
    eye = np.eye(DFT_ROWS)
    kk1 = np.arange(n1)[:, None, None]
    kk2 = np.arange(n2)[None, :, None]
    nn2 = np.arange(n2)[None, None, :]
    ang = 2.0 * np.pi * ((nn2 * (kk1 + n1 * kk2)) % n) / n
    c, s = np.cos(ang), np.sin(ang)
    g = np.concatenate([np.concatenate([c, s], axis=2), np.concatenate([-s, c], axis=2)], axis=1)
    ct, st = c.transpose(0, 2, 1), s.transpose(0, 2, 1)
    ginv = np.concatenate([np.concatenate([ct, -st], axis=2), np.concatenate([st, ct], axis=2)], axis=1)
    as_bf = lambda a: jnp.asarray(a.astype(np.float32), dtype=BF16)
    return (as_bf(np.kron(f1, eye)), as_bf(np.kron(f1[:, 0:n1 // 2], eye)),
            as_bf(np.kron(f1inv[0:n1 // 2, :], eye)), as_bf(g), as_bf(ginv))


def _hy_outer_body(f_ref, x_ref, o_ref):
    k1, rows, w = x_ref.shape[1:]
    x = x_ref[0].reshape(k1 * rows, w).astype(BF16)
    y = jnp.dot(f_ref[...], x, preferred_element_type=F32)
    o_ref[0] = y.reshape(y.shape[0] // rows, rows, w)


def _hy_outer(fk, x):
    bsz, k1, n2, w = x.shape
    m1 = fk.shape[0] // DFT_ROWS
    return pl.pallas_call(
        _hy_outer_body, grid=(bsz, n2 // DFT_ROWS),
        in_specs=[pl.BlockSpec(fk.shape, lambda bi, j: (0, 0), pipeline_mode=pl.Buffered(1)),
                  pl.BlockSpec((1, k1, DFT_ROWS, w), lambda bi, j: (bi, 0, j, 0))],
        out_specs=pl.BlockSpec((1, m1, DFT_ROWS, w), lambda bi, j: (bi, 0, j, 0)),
        out_shape=jax.ShapeDtypeStruct((bsz, m1, n2, w), F32),
        compiler_params=_cparams(("parallel", "parallel")), name="hy_outer")(fk, x)


def _hy_spec_body(g_ref, a_ref, inv_ref, h_ref):
    kb = g_ref.shape[0]
    for i in range(kb):
        h_ref[i] = jnp.dot(g_ref[i], a_ref[i].astype(BF16), preferred_element_type=F32) * inv_ref[...]


def _hy_spectrum(g, a, inv_norm, *, kb=8):
    n1, m, w = a.shape
    return pl.pallas_call(
        _hy_spec_body, grid=(n1 // kb,),
        in_specs=[pl.BlockSpec((kb, m, m), lambda i: (i, 0, 0)), pl.BlockSpec((kb, m, w), lambda i: (i, 0, 0)),
                  pl.BlockSpec((1, w), lambda i: (0, 0))],
        out_specs=pl.BlockSpec((kb, m, w), lambda i: (i, 0, 0)),
        out_shape=jax.ShapeDtypeStruct((n1, m, w), F32),
        compiler_params=_cparams(("parallel",)), name="hy_spectrum")(g, a, inv_norm)


def _hy_inner_body(g_ref, gi_ref, h_ref, a_ref, o_ref):
    kb = g_ref.shape[0]
    n2 = g_ref.shape[1] // 2
    xs = [jnp.dot(g_ref[i], a_ref[0, i].astype(BF16), preferred_element_type=F32) for i in range(kb)]
    zs = []
    for i in range(kb):
        xr, xi = xs[i][0:n2], xs[i][n2:]
        hr, hi = h_ref[i, 0:n2, :], h_ref[i, n2:, :]
        zs.append(jnp.concatenate([xr * hr - xi * hi, xr * hi + xi * hr], axis=0).astype(BF16))
    for i in range(kb):
        o_ref[0, i] = jnp.dot(gi_ref[i], zs[i], preferred_element_type=F32)


def _hy_inner(g, ginv, hspec, a, order, *, kb=8):
    bsz, n1, m, c = a.shape
    return pl.pallas_call(
        _hy_inner_body, grid=(n1 // kb, bsz),
        in_specs=[pl.BlockSpec((kb, m, m), lambda i, bi: (i, 0, 0)), pl.BlockSpec((kb, m, m), lambda i, bi: (i, 0, 0)),
                  pl.BlockSpec((kb, m, c), lambda i, bi: (i, 0, order)),
                  pl.BlockSpec((1, kb, m, c), lambda i, bi: (bi, i, 0, 0))],
        out_specs=pl.BlockSpec((1, kb, m, c), lambda i, bi: (bi, i, 0, 0)),
        out_shape=jax.ShapeDtypeStruct((bsz, n1, m, c), F32),
        compiler_params=_cparams(("parallel", "parallel")), name="hy_inner")(g, ginv, hspec, a)


def _hy_gate_body(f_ref, a_ref, u_ref, x_ref, d_ref, o_ref):
    k1, rows, w = a_ref.shape[1:]
    a = a_ref[0].reshape(k1 * rows, w).astype(BF16)
    y = jnp.dot(f_ref[...], a, preferred_element_type=F32)
    y = y.reshape(y.shape[0] // rows, rows, w)
    o_ref[0] = x_ref[0] * (y + u_ref[0] * d_ref[...])


def _hy_gate(fk_inv, a, u, xg, dskip):
    bsz, k1, n2, w = a.shape
    m1 = fk_inv.shape[0] // DFT_ROWS
    sig = pl.BlockSpec((1, m1, DFT_ROWS, w), lambda bi, j: (bi, 0, j, 0))
    return pl.pallas_call(
        _hy_gate_body, grid=(bsz, n2 // DFT_ROWS),
        in_specs=[pl.BlockSpec(fk_inv.shape, lambda bi, j: (0, 0), pipeline_mode=pl.Buffered(1)),
                  pl.BlockSpec((1, k1, DFT_ROWS, w), lambda bi, j: (bi, 0, j, 0)),
                  sig, sig, pl.BlockSpec((1, w), lambda bi, j: (0, 0))],
        out_specs=sig,
        out_shape=jax.ShapeDtypeStruct((bsz, m1, n2, w), F32),
        compiler_params=_cparams(("parallel", "parallel")), name="hy_gate")(fk_inv, a, u, xg, dskip)


def _hyena(pc, lp, *, batch, seq_len):
    ln = seq_len
    n = 2 * ln
    n1, n2 = _dft_split(n)
    fk_full, fk_half, fk_inv, g, ginv = _dft_tables(n)
    v, x1, x2 = _hy_short(pc, lp, seq_len=ln)
    kern, l1 = _hy_filter(lp, ln)
    ka = _hy_outer(fk_full, kern.reshape(1, n1, n2, HY_ORDER * C_DIM))
    hspec = _hy_spectrum(g, ka.reshape(n1, 2 * n2, HY_ORDER * C_DIM), 1.0 / l1)
    view = lambda t: t.reshape(batch, n1 // 2, n2, C_DIM)

    def conv_gate(u, xg, order):
        a = _hy_outer(fk_half, u)
        a = _hy_inner(g, ginv, hspec, a.reshape(batch, n1, 2 * n2, C_DIM), order)
        return _hy_gate(fk_inv, a.reshape(batch, 2 * n1, n2, C_DIM), u, xg, lp['hy_bias'][order:order + 1])

    z = conv_gate(view(v), view(x1), 0)
    y = conv_gate(z, view(x2), 1)
    return y.reshape(batch * ln, C_DIM)


def _trunk(x, params, final_norm):
    batch, seq_len, _ = x.shape
    x = x.reshape(batch * seq_len, D_MODEL)
    mix = None
    for layer in range(DEPTH):
        lp = {name: arr[layer] for name, arr in params.items()}
        if mix is not None:
            prev = {name: arr[layer - 1] for name, arr in params.items()}
            x = _ffn(x, prev['norm_ffn2'], prev['ffn2_w_in'].astype(BF16), prev['ffn2_w_out'].astype(BF16), mix=mix)
        x = _ffn(x, lp['norm_ffn1'], lp['ffn1_w_in'].astype(BF16), lp['ffn1_w_out'].astype(BF16))
        w = lp['w_in']
        w_pad = jnp.concatenate([w[:, :A_COLS], jnp.zeros((D_MODEL, A_PAD - A_COLS), F32), w[:, A_COLS:]], axis=1)
        pa, pb, pc = _proj(x, lp['norm_mix'], w_pad.astype(BF16))
        prep = _rwkv_prep(pa, lp, seq_len=seq_len)
        yf, yb = _wkv(prep, lp, batch=batch, seq_len=seq_len)
        nb = _natten(pb, lp['na_rpb'], lp['na_out_norm'], batch=batch, seq_len=seq_len)
        yc = _hyena(pc, lp, batch=batch, seq_len=seq_len)
        mix = (yf, yb, prep[3], nb, yc, lp['w_out'].astype(BF16), lp['hy_out_norm'])
    last = {name: arr[DEPTH - 1] for name, arr in params.items()}
    x = _ffn(x, last['norm_ffn2'], last['ffn2_w_in'].astype(BF16), last['ffn2_w_out'].astype(BF16), mix=mix,
             final_g=final_norm)
    return x.reshape(batch, seq_len, D_MODEL)


def kernel(x_prompt, x_sample, norm_ffn1, ffn1_w_in, ffn1_w_out, norm_mix, w_in, rwkv_mu, rwkv_w0, rwkv_w_up, rwkv_a0, rwkv_a_up, rwkv_g_up, rwkv_k_k, rwkv_k_a, rwkv_r_k, rwkv_ln_g, rwkv_ln_b, na_rpb, na_out_norm, hy_conv_w, hy_conv_b, hy_ff_w1, hy_ff_b1, hy_ff_w2, hy_ff_b2, hy_ff_w3, hy_sin_freq, hy_bias, hy_out_norm, w_out, norm_ffn2, ffn2_w_in, ffn2_w_out, final_norm):
    params = {
        'norm_ffn1': norm_ffn1, 'ffn1_w_in': ffn1_w_in, 'ffn1_w_out': ffn1_w_out,
        'norm_mix': norm_mix, 'w_in': w_in,
        'rwkv_mu': rwkv_mu, 'rwkv_w0': rwkv_w0, 'rwkv_w_up': rwkv_w_up, 'rwkv_a0': rwkv_a0,
        'rwkv_a_up': rwkv_a_up, 'rwkv_g_up': rwkv_g_up, 'rwkv_k_k': rwkv_k_k, 'rwkv_k_a': rwkv_k_a,
        'rwkv_r_k': rwkv_r_k.reshape(DEPTH, A_DIM), 'rwkv_ln_g': rwkv_ln_g, 'rwkv_ln_b': rwkv_ln_b,
        'na_rpb': na_rpb, 'na_out_norm': na_out_norm,
        'hy_conv_w': hy_conv_w, 'hy_conv_b': hy_conv_b, 'hy_ff_w1': hy_ff_w1, 'hy_ff_b1': hy_ff_b1,
        'hy_ff_w2': hy_ff_w2, 'hy_ff_b2': hy_ff_b2, 'hy_ff_w3': hy_ff_w3, 'hy_sin_freq': hy_sin_freq,
        'hy_bias': hy_bias, 'hy_out_norm': hy_out_norm,
        'w_out': w_out, 'norm_ffn2': norm_ffn2, 'ffn2_w_in': ffn2_w_in, 'ffn2_w_out': ffn2_w_out,
    }
    return (_trunk(x_prompt, params, final_norm), _trunk(x_sample, params, final_norm))
```

```python
import functools
import math

import numpy as np
import jax
import jax.numpy as jnp
from jax import lax
from jax.experimental import pallas as pl
from jax.experimental.pallas import tpu as pltpu

F32 = jnp.float32
BF16 = jnp.bfloat16

D_MODEL = 1024
DEPTH = 4
GRID_W = 64
HEAD_DIM = 64
A_HEADS = 4
A_DIM = A_HEADS * HEAD_DIM
A_LORA_W = 32
A_LORA_A = 32
A_LORA_G = 64
B_HEADS = 8
B_DIM = B_HEADS * HEAD_DIM
NA_KH_MAX = 8
NA_KW = 16
C_DIM = D_MODEL - A_DIM - B_DIM
C_GROUPS = C_DIM // HEAD_DIM
HY_ORDER = 2
HY_BANDS = 16
HY_EMB = 1 + 2 * HY_BANDS
HY_FF = 64
HY_FAST_DECAY_PCT = 0.3
HY_SLOW_DECAY_PCT = 1.5
HY_TARGET = 1e-2
D_FF = 2816
NORM_EPS = 1e-5
GN_EPS = 64e-5
A_COLS = 3 * A_DIM + 2 * A_LORA_W + 2 * A_LORA_A + A_LORA_G
A_PAD = 1024
B_COLS = 3 * B_DIM
C_COLS = (HY_ORDER + 1) * C_DIM

SUBLANES = 8
VMEM_LIMIT = 56 * 1024 * 1024
HY_VMEM_LIMIT = 60 * 1024 * 1024
WKV_CHUNK = 64
NEG_BIG = -1e30
DFT_ROWS = SUBLANES


def _cparams(sem):
    return pltpu.CompilerParams(dimension_semantics=sem, vmem_limit_bytes=VMEM_LIMIT)


def _const_spec(shape):
    nd = len(shape)
    return pl.BlockSpec(shape, lambda *_: (0,) * nd, pipeline_mode=pl.Buffered(1))


def _dot(a, b):
    return jnp.dot(a.astype(BF16), b.astype(BF16), preferred_element_type=F32)


def _dot_nt(a, b):
    return lax.dot_general(a.astype(BF16), b.astype(BF16), (((1,), (1,)), ((), ())),
                           preferred_element_type=F32)


def _dot_tn(a, b):
    return lax.dot_general(a.astype(BF16), b.astype(BF16), (((0,), (0,)), ((), ())),
                           preferred_element_type=F32)


def _split(a):
    hi = a.astype(BF16)
    lo = (a - hi.astype(F32)).astype(BF16)
    return hi, lo


def _dot_lhs2(a, b_bf16):
    hi, lo = _split(a)
    return (jnp.dot(hi, b_bf16, preferred_element_type=F32)
            + jnp.dot(lo, b_bf16, preferred_element_type=F32))


def _dot3(a, b):
    ah, al = _split(a)
    bh, bl = _split(b)
    return (jnp.dot(ah, bh, preferred_element_type=F32)
            + jnp.dot(al, bh, preferred_element_type=F32)
            + jnp.dot(ah, bl, preferred_element_type=F32))


def _rms(x, g):
    return x * lax.rsqrt(jnp.mean(x * x, axis=-1, keepdims=True) + NORM_EPS) * g


def _group_mat(n, group, value):
    idx = np.arange(n) // group
    return jnp.asarray((idx[:, None] == idx[None, :]).astype(np.float32) * value, dtype=BF16)


def _ffn_body(*refs, n_chunks, with_mix, with_final):
    it = iter(refs)
    x_ref = next(it)
    if with_mix:
        yf_ref, yb_ref, g_ref, nb_ref, yc_ref, wo_ref, cn_ref, pavg_ref = (next(it) for _ in range(8))
    ng_ref, win_ref, wout_ref = next(it), next(it), next(it)
    if with_final:
        fn_ref = next(it)
    o_ref = next(it)

    x = x_ref[...]
    if with_mix:
        ya = (yf_ref[...] + yb_ref[...]) * g_ref[...]
        yc = yc_ref[...]
        ms = _dot_lhs2(yc * yc, pavg_ref[...])
        yc = yc * lax.rsqrt(ms + NORM_EPS) * cn_ref[...]
        x = (x + _dot(ya, wo_ref[0:A_DIM, :]) + _dot(nb_ref[...], wo_ref[A_DIM:A_DIM + B_DIM, :])
             + _dot(yc, wo_ref[A_DIM + B_DIM:, :]))
    h = _rms(x, ng_ref[...]).astype(BF16)
    cw = D_FF // n_chunks
    acc = jnp.zeros_like(x)
    for c in range(n_chunks):
        gate = jnp.dot(h, win_ref[:, c * cw:(c + 1) * cw], preferred_element_type=F32)
        up = jnp.dot(h, win_ref[:, D_FF + c * cw:D_FF + (c + 1) * cw], preferred_element_type=F32)
        act = (gate * jax.nn.sigmoid(gate) * up).astype(BF16)
        acc = acc + jnp.dot(act, wout_ref[c * cw:(c + 1) * cw, :], preferred_element_type=F32)
    y = x + 0.5 * acc
    if with_final:
        y = _rms(y, fn_ref[...])
    o_ref[...] = y


def _ffn(x, norm_g, w_in, w_out, *, mix=None, final_g=None, tm=512, n_chunks=2):
    n = x.shape[0]
    tok = lambda w: pl.BlockSpec((tm, w), lambda i: (i, 0))
    args, specs = [x], [tok(D_MODEL)]
    if mix is not None:
        yf, yb, g, nb, yc, wo, cn = mix
        args += [yf, yb, g, nb, yc, wo, cn.reshape(1, C_DIM), _group_mat(C_DIM, HEAD_DIM, 1.0 / HEAD_DIM)]
        specs += [tok(A_DIM), tok(A_DIM), tok(A_DIM), tok(B_DIM), tok(C_DIM),
                  _const_spec((D_MODEL, D_MODEL)), _const_spec((1, C_DIM)), _const_spec((C_DIM, C_DIM))]
    args += [norm_g.reshape(1, D_MODEL), w_in, w_out]
    specs += [_const_spec((1, D_MODEL)), _const_spec((D_MODEL, 2 * D_FF)), _const_spec((D_FF, D_MODEL))]
    if final_g is not None:
        args.append(final_g.reshape(1, D_MODEL))
        specs.append(_const_spec((1, D_MODEL)))
    body = functools.partial(_ffn_body, n_chunks=n_chunks, with_mix=mix is not None,
                             with_final=final_g is not None)
    return pl.pallas_call(
        body, grid=(n // tm,), in_specs=specs, out_specs=tok(D_MODEL),
        out_shape=jax.ShapeDtypeStruct((n, D_MODEL), F32),
        compiler_params=_cparams(("parallel",)), name="ffn")(*args)


def _proj_body(x_ref, g_ref, w_ref, pa_ref, pb_ref, pc_ref):
    h = _rms(x_ref[...], g_ref[...]).astype(BF16)
    pa_ref[...] = jnp.dot(h, w_ref[:, 0:A_PAD], preferred_element_type=F32)
    pb_ref[...] = jnp.dot(h, w_ref[:, A_PAD:A_PAD + B_COLS], preferred_element_type=F32).astype(BF16)
    pc_ref[...] = jnp.dot(h, w_ref[:, A_PAD + B_COLS:], preferred_element_type=F32)


def _proj(x, norm_g, w_pad, *, tm=512):
    n = x.shape[0]
    tok = lambda w: pl.BlockSpec((tm, w), lambda i: (i, 0))
    wcols = A_PAD + B_COLS + C_COLS
    return pl.pallas_call(
        _proj_body, grid=(n // tm,),
        in_specs=[tok(D_MODEL), _const_spec((1, D_MODEL)), _const_spec((D_MODEL, wcols))],
        out_specs=[tok(A_PAD), tok(B_COLS), tok(C_COLS)],
        out_shape=[jax.ShapeDtypeStruct((n, A_PAD), F32), jax.ShapeDtypeStruct((n, B_COLS), BF16),
                   jax.ShapeDtypeStruct((n, C_COLS), F32)],
        compiler_params=_cparams(("parallel",)), name="proj")(x, norm_g.reshape(1, D_MODEL), w_pad)


def _halo_specs(tm, width, n_tok):
    per = tm // SUBLANES
    last = n_tok // SUBLANES - 1
    prev = pl.BlockSpec((SUBLANES, width), lambda i: (jnp.maximum(i * per - 1, 0), 0))
    nxt = pl.BlockSpec((SUBLANES, width), lambda i: (jnp.minimum((i + 1) * per, last), 0))
    return prev, nxt


def _shifted(p, prev_ref, next_ref, blocks_per_seq):
    tm = p.shape[0]
    j = pl.program_id(0) % blocks_per_seq
    hp = jnp.where(j == 0, 0.0, prev_ref[SUBLANES - 1:SUBLANES, :])
    hn = jnp.where(j == blocks_per_seq - 1, 0.0, next_ref[0:1, :])
    row = lax.broadcasted_iota(jnp.int32, p.shape, 0)
    prev = jnp.where(row == 0, hp, pltpu.roll(p, 1, axis=0))
    nxt = jnp.where(row == tm - 1, hn, pltpu.roll(p, tm - 1, axis=0))
    return prev, nxt


def _rwkv_prep_body(p_ref, pp_ref, pn_ref, mu_ref, wlo_ref, w0a0_ref, gup_ref, kk_ref, ka_ref, rk_ref,
                    hsum_ref, r_o, v_o, kk_o, g_o, lw0_o, lw1_o, kd0_o, kd1_o, b0_o, b1_o, bv0_o, bv1_o,
                    *, blocks_per_seq):
    p = p_ref[...]
    prev, nxt = _shifted(p, pp_ref, pn_ref, blocks_per_seq)
    p = p + mu_ref[...] * (0.5 * (prev + nxt) - p)
    r = p[:, 0:A_DIM]
    k = p[:, A_DIM:2 * A_DIM]
    v = p[:, 2 * A_DIM:3 * A_DIM]
    lo = p[:, 3 * A_DIM:3 * A_DIM + 128]
    gd = p[:, 3 * A_DIM + 128:3 * A_DIM + 256]
    lane = lax.broadcasted_iota(jnp.int32, lo.shape, 1)
    lo = jnp.where(lane < 2 * A_LORA_W, jnp.tanh(lo), lo)
    wa = _dot3(lo, wlo_ref[...]) + w0a0_ref[...]
    g = _dot3(jax.nn.sigmoid(gd), gup_ref[...])
    hsum = hsum_ref[...]
    kkv = k * kk_ref[...]
    nrm = jnp.sqrt(_dot_lhs2(kkv * kkv, hsum))
    kkv = kkv / jnp.maximum(nrm, 1e-12)
    r_o[...] = r
    v_o[...] = v
    kk_o[...] = kkv
    g_o[...] = g
    for d, (lw_o, kd_o, b_o, bv_o) in enumerate(((lw0_o, kd0_o, b0_o, bv0_o), (lw1_o, kd1_o, b1_o, bv1_o))):
        w_log = -jax.nn.softplus(-wa[:, d * A_DIM:(d + 1) * A_DIM]) - 0.5
        a = jax.nn.sigmoid(wa[:, (2 + d) * A_DIM:(3 + d) * A_DIM])
        kd = k * (1.0 + (a - 1.0) * ka_ref[...])
        lw_o[...] = -jnp.exp(w_log)
        kd_o[...] = kd
        b_o[...] = kkv * a
        bv_o[...] = _dot_lhs2(r * kd * rk_ref[...], hsum) * v


def _rwkv_prep(pa, lp, *, seq_len, tm=512):
    n = pa.shape[0]
    tok = lambda w: pl.BlockSpec((tm, w), lambda i: (i, 0))
    prev, nxt = _halo_specs(tm, A_PAD, n)
    mu = jnp.pad(lp['rwkv_mu'], (0, A_PAD - A_COLS)).reshape(1, A_PAD)
    wlo = jnp.zeros((128, 4 * A_DIM), F32)
    for d in range(2):
        wlo = wlo.at[d * A_LORA_W:(d + 1) * A_LORA_W, d * A_DIM:(d + 1) * A_DIM].set(lp['rwkv_w_up'][d])
        wlo = wlo.at[64 + d * A_LORA_A:64 + (d + 1) * A_LORA_A, (2 + d) * A_DIM:(3 + d) * A_DIM].set(lp['rwkv_a_up'][d])
    w0a0 = jnp.concatenate([lp['rwkv_w0'].reshape(-1), lp['rwkv_a0'].reshape(-1)]).reshape(1, 4 * A_DIM)
    gup = jnp.pad(lp['rwkv_g_up'], ((0, 128 - A_LORA_G), (0, 0)))
    consts = [mu, wlo, w0a0, gup, lp['rwkv_k_k'].reshape(1, A_DIM), lp['rwkv_k_a'].reshape(1, A_DIM),
              lp['rwkv_r_k'].reshape(1, A_DIM), _group_mat(A_DIM, HEAD_DIM, 1.0)]
    body = functools.partial(_rwkv_prep_body, blocks_per_seq=seq_len // tm)
    return pl.pallas_call(
        body, grid=(n // tm,),
        in_specs=[tok(A_PAD), prev, nxt] + [_const_spec(c.shape) for c in consts],
        out_specs=[tok(A_DIM)] * 12,
        out_shape=[jax.ShapeDtypeStruct((n, A_DIM), F32)] * 12,
        compiler_params=_cparams(("parallel",)), name="rwkv_prep")(pa, pa, pa, *consts)


def _wkv_setup(r, v, kk, lw, kd, b, reverse):
    tb = r.shape[0]
    c_len = WKV_CHUNK
    pos = lax.broadcasted_iota(jnp.int32, (tb, A_DIM), 0) % c_len
    cum = lw
    sh = 1
    while sh < c_len:
        if reverse:
            cum = cum + jnp.where(pos < c_len - sh, pltpu.roll(cum, tb - sh, axis=0), 0.0)
        else:
            cum = cum + jnp.where(pos >= sh, pltpu.roll(cum, sh, axis=0), 0.0)
        sh *= 2
    e_neg = jnp.exp(-cum)
    return dict(cum=cum, rq=r * jnp.exp(cum), kq=kk * jnp.exp(cum - lw), kbar=kd * e_neg, bbar=b * e_neg, v=v)


def _wkv_body(rf, vf, kkf, lwf, kdf, bf, bvf, rb, vb, kkb, lwb, kdb, bb, bvb, lng_ref, lnb_ref, pavg_ref,
              of_ref, ob_ref, sf_ref, sb_ref):
    @pl.when(pl.program_id(1) == 0)
    def _():
        sf_ref[...] = jnp.zeros_like(sf_ref)
        sb_ref[...] = jnp.zeros_like(sb_ref)

    c_len = WKV_CHUNK
    tb = rf.shape[0]
    nc = tb // c_len
    dirs = (_wkv_setup(rf[...], vf[...], kkf[...], lwf[...], kdf[...], bf[...], False),
            _wkv_setup(rb[...], vb[...], kkb[...], lwb[...], kdb[...], bb[...], True))
    s_refs, o_refs, bvs = (sf_ref, sb_ref), (of_ref, ob_ref), (bvf, bvb)

    t_idx = lax.broadcasted_iota(jnp.int32, (c_len, A_DIM), 0)
    s_idx = lax.broadcasted_iota(jnp.int32, (c_len, A_DIM), 1) % c_len
    incl = (s_idx <= t_idx, s_idx >= t_idx)
    strict = (s_idx < t_idx, s_idx > t_idx)
    eye = (s_idx == t_idx).astype(F32)
    bd_mask = (lax.broadcasted_iota(jnp.int32, (A_DIM, A_DIM), 0) // HEAD_DIM
               == lax.broadcasted_iota(jnp.int32, (A_DIM, A_DIM), 1) // HEAD_DIM)
    zero_bf = jnp.zeros((A_DIM, A_DIM), BF16)

    def stack_hl(x_hl):
        return tuple(jnp.where(bd_mask, jnp.concatenate([part] * A_HEADS, axis=0), zero_bf) for part in x_hl)

    def stack(x):
        return stack_hl(_split(x))

    def mm_hl(a_hl, b_hl, nt=False):
        bh, bl = b_hl
        ah, al = a_hl
        m = ah.shape[0]
        dg = _dot_nt if nt else _dot
        p = dg(jnp.concatenate([ah, al], axis=0), bh)
        return p[0:m] + p[m:] + dg(ah, bl)

    def mm(a, b_hl, nt=False):
        return mm_hl(_split(a), b_hl, nt)

    items = [(d, c) for c in range(nc) for d in (0, 1)]
    ch = {}
    for d, c in items:
        sl = slice(c * c_len, (c + 1) * c_len)
        q = dirs[d]
        ch[d, c] = dict(rq=q['rq'][sl], kq=q['kq'][sl], kbar=q['kbar'][sl], bbar=q['bbar'][sl], v=q['v'][sl])
    for key in items:
        e = ch[key]
        kst, bst = stack(e['kbar']), stack(e['bbar'])
        acat = mm(jnp.concatenate([e['rq'], e['kq']], axis=0),
                  (jnp.concatenate([kst[0], bst[0]], axis=0), jnp.concatenate([kst[1], bst[1]], axis=0)),
                  nt=True)
        d = key[0]
        e['a_rv'] = jnp.concatenate([jnp.where(incl[d], acat[0:c_len, 0:A_DIM], 0.0),
                                     jnp.where(strict[d], acat[c_len:, 0:A_DIM], 0.0)], axis=0)
        e['a_rb'] = jnp.where(incl[d], acat[0:c_len, A_DIM:], 0.0)
        e['a_kb'] = jnp.where(strict[d], acat[c_len:, A_DIM:], 0.0)
        e['a_kb_hl'] = _split(e['a_kb'])
    half = 1
    zero_c = jnp.zeros((c_len, A_DIM), BF16)
    while half < c_len:
        in_block = (t_idx // (2 * half)) == (s_idx // (2 * half))
        lo_t, lo_s = t_idx % (2 * half) < half, s_idx % (2 * half) < half
        off = (in_block & ~lo_t & lo_s, in_block & lo_t & ~lo_s)
        if half == 1:
            for key in items:
                ch[key]['tinv'] = eye - jnp.where(off[key[0]], ch[key]['a_kb'], 0.0)
        else:
            for key in items:
                e = ch[key]
                e['tinv_hl'] = _split(e['tinv'])
                a_off_hl = tuple(jnp.where(off[key[0]], part, zero_c) for part in e['a_kb_hl'])
                e['tmp'] = mm_hl(e['tinv_hl'], stack_hl(a_off_hl))
            for key in items:
                e = ch[key]
                e['tinv'] = e['tinv'] - mm(e['tmp'], stack_hl(e['tinv_hl']))
        half *= 2
    for key in items:
        e = ch[key]
        av = mm(e['a_rv'], stack(e['v']))
        e['o_intra'], e['akkv'] = av[0:c_len], av[c_len:]
    for key in items:
        e = ch[key]
        t_hl = _split(e['tinv'])
        e['w_t'] = mm_hl(t_hl, stack(e['kq']))
        e['u0'] = mm_hl(t_hl, stack(e['akkv']))
    outs = ([None] * nc, [None] * nc)
    for step in range(nc):
        cur = [(0, step), (1, nc - 1 - step)]
        xs = {}
        for d, c in cur:
            e = ch[d, c]
            e['s_old'] = s_refs[d][...]
            xs[d] = mm(jnp.concatenate([e['rq'], e['w_t']], axis=0), _split(e['s_old']), nt=True)
        for d, c in cur:
            e = ch[d, c]
            u = -(xs[d][c_len:] + e['u0'])
            edge = c * c_len if d else (c + 1) * c_len - 1
            e_end = jnp.exp(dirs[d]['cum'][edge:edge + 1, :])
            vu_h, vu_l = _split(jnp.concatenate([e['v'], u], axis=0))
            kb_h, kb_l = _split(jnp.concatenate([e['kbar'] * e_end, e['bbar'] * e_end], axis=0))
            ds = _dot_tn(jnp.concatenate([vu_h, vu_l, vu_h], axis=0), jnp.concatenate([kb_h, kb_h, kb_l], axis=0))
            s_refs[d][...] = e['s_old'] * e_end + jnp.where(bd_mask, ds, 0.0)
            outs[d][c] = xs[d][0:c_len] + e['o_intra'] + mm(e['a_rb'], stack(u))
    lng, lnb, pavg = lng_ref[...], lnb_ref[...], pavg_ref[...]
    for d in (0, 1):
        o = jnp.concatenate(outs[d], axis=0)
        mu = _dot_lhs2(o, pavg)
        dev = o - mu
        var = _dot_lhs2(dev * dev, pavg)
        o_refs[d][...] = dev * lax.rsqrt(var + GN_EPS) * lng + lnb + bvs[d][...]


def _wkv(prep, lp, *, batch, seq_len, tb=512):
    r, v, kk, _, lw0, lw1, kd0, kd1, b0, b1, bv0, bv1 = prep
    n = r.shape[0]
    nblk = seq_len // tb
    fwd = pl.BlockSpec((tb, A_DIM), lambda bi, j: (bi * nblk + j, 0))
    bwd = pl.BlockSpec((tb, A_DIM), lambda bi, j: (bi * nblk + nblk - 1 - j, 0))
    consts = [lp['rwkv_ln_g'].reshape(1, A_DIM), lp['rwkv_ln_b'].reshape(1, A_DIM),
              _group_mat(A_DIM, HEAD_DIM, 1.0 / HEAD_DIM)]
    cspec = lambda shape: pl.BlockSpec(shape, lambda bi, j: (0, 0), pipeline_mode=pl.Buffered(1))
    return pl.pallas_call(
        _wkv_body, grid=(batch, nblk),
        in_specs=[fwd] * 7 + [bwd] * 7 + [cspec(c.shape) for c in consts],
        out_specs=[fwd, bwd],
        out_shape=[jax.ShapeDtypeStruct((n, A_DIM), F32)] * 2,
        scratch_shapes=[pltpu.VMEM((A_DIM, A_DIM), F32), pltpu.VMEM((A_DIM, A_DIM), F32)],
        compiler_params=_cparams(("parallel", "arbitrary")), name="wkv")(
            r, v, kk, lw0, kd0, b0, bv0, r, v, kk, lw1, kd1, b1, bv1, *consts)


def _na_tables(rpb, kh):
    col = np.arange(GRID_W)
    col_start = np.clip(col - NA_KW // 2, 0, GRID_W - NA_KW)
    col_ok = (col[None, :] >= col_start[:, None]) & (col[None, :] < col_start[:, None] + NA_KW)
    dx_idx = np.clip(col[None, :] - col[:, None] + NA_KW - 1, 0, 2 * NA_KW - 2)
    onehot = ((np.arange(2 * NA_KW - 1)[:, None, None] == dx_idx[None]) & col_ok[None]).astype(np.float32)
    mask_add = np.where(col_ok, 0.0, NEG_BIG).astype(np.float32)
    rpbx = jnp.einsum('hdx,xqk->hdqk', rpb.astype(F32), jnp.asarray(onehot),
                      precision=lax.Precision.HIGHEST) + jnp.asarray(mask_add)
    tabs = []
    for delta in range(kh):
        lo = NA_KH_MAX - 1 - delta
        tab = jnp.transpose(rpbx[:, lo:lo + kh], (0, 2, 1, 3))
        tabs.append(tab.reshape(B_HEADS * GRID_W, kh * GRID_W))
    return jnp.stack(tabs)


def _na_body(*refs, rows, kh, rq):
    q_ref, k_ref, v_ref = refs[0:3]
    bm_refs = refs[3:3 + rq]
    gn_ref, pavg_ref, o_ref = refs[3 + rq:]
    n_pairs = B_HEADS // 2
    pw = 2 * HEAD_DIM
    second = lax.broadcasted_iota(jnp.int32, (GRID_W, pw), 1) >= HEAD_DIM
    wins, ss = [], []
    for i in range(rq):
        r = pl.program_id(1) * rq + i
        rs = jnp.clip(r - kh // 2, 0, rows - kh)
        start = pl.multiple_of(rs * GRID_W, GRID_W)
        wins.append(start)
        parts = []
        for p in range(n_pairs):
            qp = q_ref[0, i * GRID_W:(i + 1) * GRID_W, p * pw:(p + 1) * pw]
            zero = jnp.zeros_like(qp)
            qs = jnp.concatenate([jnp.where(second, zero, qp), jnp.where(second, qp, zero)], axis=0)
            parts.append(_dot_nt(qs, k_ref[0, pl.ds(start, kh * GRID_W), p * pw:(p + 1) * pw]))
        ss.append(jnp.concatenate(parts, axis=0))
    ps, ls = [], []
    for i in range(rq):
        s = ss[i] * (HEAD_DIM ** -0.5) + bm_refs[i][0]
        m = jnp.max(s, axis=-1, keepdims=True)
        p = jnp.exp(s - m)
        ls.append(jnp.sum(p, axis=-1, keepdims=True))
        ps.append(p.astype(BF16))
    outs = []
    for i in range(rq):
        cols = []
        for p in range(n_pairs):
            rows_p = slice(p * 2 * GRID_W, (p + 1) * 2 * GRID_W)
            pv = _dot(ps[i][rows_p], v_ref[0, pl.ds(wins[i], kh * GRID_W), p * pw:(p + 1) * pw]) / ls[i][rows_p]
            cols.append(jnp.where(second, pv[GRID_W:], pv[0:GRID_W]))
        outs.append(jnp.concatenate(cols, axis=1))
    out = jnp.concatenate(outs, axis=0)
    ms = _dot_lhs2(out * out, pavg_ref[...])
    o_ref[0] = (out * lax.rsqrt(ms + NORM_EPS) * gn_ref[...]).astype(o_ref.dtype)


def _natten(pb, rpb, out_norm, *, batch, seq_len, rq=4):
    rows = seq_len // GRID_W
    kh = min(NA_KH_MAX, rows)
    tables = _na_tables(rpb, kh)
    pb3 = pb.reshape(batch, seq_len, B_COLS)

    def delta_map(i):
        def index(bi, j):
            r = j * rq + i
            return (r - jnp.clip(r - kh // 2, 0, rows - kh), 0, 0)
        return index

    body = functools.partial(_na_body, rows=rows, kh=kh, rq=rq)
    out = pl.pallas_call(
        body, grid=(batch, rows // rq),
        in_specs=[pl.BlockSpec((1, rq * GRID_W, B_DIM), lambda bi, j: (bi, j, 0)),
                  pl.BlockSpec((1, seq_len, B_DIM), lambda bi, j: (bi, 0, 1)),
                  pl.BlockSpec((1, seq_len, B_DIM), lambda bi, j: (bi, 0, 2))]
                 + [pl.BlockSpec((1, B_HEADS * GRID_W, kh * GRID_W), delta_map(i)) for i in range(rq)]
                 + [pl.BlockSpec((1, B_DIM), lambda bi, j: (0, 0)),
                    pl.BlockSpec((B_DIM, B_DIM), lambda bi, j: (0, 0))],
        out_specs=pl.BlockSpec((1, rq * GRID_W, B_DIM), lambda bi, j: (bi, j, 0)),
        out_shape=jax.ShapeDtypeStruct((batch, seq_len, B_DIM), BF16),
        compiler_params=_cparams(("parallel", "arbitrary")), name="natten")(
            pb3, pb3, pb3, *([tables] * rq), out_norm.reshape(1, B_DIM),
            _group_mat(B_DIM, HEAD_DIM, 1.0 / HEAD_DIM))
    return out.reshape(batch * seq_len, B_DIM)


def _hy_short_body(p_ref, pp_ref, pn_ref, cw_ref, cb_ref, v_o, x1_o, x2_o, *, blocks_per_seq):
    p = p_ref[...]
    prev, nxt = _shifted(p, pp_ref, pn_ref, blocks_per_seq)
    y = prev * cw_ref[0:1, :] + p * cw_ref[1:2, :] + nxt * cw_ref[2:3, :] + cb_ref[...]
    v_o[...] = y[:, 0:C_DIM]
    x1_o[...] = y[:, C_DIM:2 * C_DIM]
    x2_o[...] = y[:, 2 * C_DIM:]


def _hy_short(pc, lp, *, seq_len, tm=512):
    n = pc.shape[0]
    tok = lambda w: pl.BlockSpec((tm, w), lambda i: (i, 0))
    prev, nxt = _halo_specs(tm, C_COLS, n)
    cw = jnp.pad(lp['hy_conv_w'], ((0, SUBLANES - 3), (0, 0)))
    body = functools.partial(_hy_short_body, blocks_per_seq=seq_len // tm)
    return pl.pallas_call(
        body, grid=(n // tm,),
        in_specs=[tok(C_COLS), prev, nxt, _const_spec((SUBLANES, C_COLS)), _const_spec((1, C_COLS))],
        out_specs=[tok(C_DIM)] * 3,
        out_shape=[jax.ShapeDtypeStruct((n, C_DIM), F32)] * 3,
        compiler_params=_cparams(("parallel",)), name="hy_short")(
            pc, pc, pc, cw, lp['hy_conv_b'].reshape(1, C_COLS))


def _hy_filter_body(z_ref, t_ref, w1_ref, b1_ref, w2_ref, b2_ref, w3_ref, fr_ref, dl_ref, k_o, s_o):
    fr = fr_ref[...]
    h = jnp.sin(fr * (_dot3(z_ref[...], w1_ref[...]) + b1_ref[...]))
    h = jnp.sin(fr * (_dot3(h, w2_ref[...]) + b2_ref[...]))
    h = _dot3(h, w3_ref[...])
    tcol = t_ref[...]
    kern = h * jnp.exp(-tcol[:, 0:1] * jnp.abs(dl_ref[...])) * tcol[:, 1:2]
    k_o[...] = kern

    @pl.when(pl.program_id(0) == 0)
    def _():
        s_o[...] = jnp.zeros_like(s_o)

    s_o[...] += jnp.sum(jnp.abs(kern), axis=0, keepdims=True)


def _hy_filter(lp, seq_len, *, rb=512):
    ln = seq_len
    pos = np.concatenate([np.arange(ln), np.array([0]), np.arange(ln - 1, 0, -1)]).astype(np.float64)
    valid = np.ones(2 * ln)
    valid[ln] = 0.0
    t32 = np.linspace(0.0, 1.0, ln, dtype=np.float32)[:, None]
    w32 = (np.float32(2.0 * math.pi) * np.arange(ln, dtype=np.float32)[:, None] / np.float32(ln)).astype(np.float32)
    f32 = np.linspace(1e-4, HY_BANDS - 1, HY_BANDS, dtype=np.float32)[None, :]
    z32 = np.concatenate([t32, np.cos(f32 * w32), -np.sin(f32 * w32)], axis=-1).astype(np.float32)
    idx = pos.astype(np.int64)
    z = np.zeros((2 * ln, 64), np.float32)
    z[:, :HY_EMB] = z32[idx]
    tcol = np.stack([t32[idx, 0], valid.astype(np.float32)], axis=-1)
    max_decay = math.log(HY_TARGET) / HY_FAST_DECAY_PCT
    min_decay = math.log(HY_TARGET) / HY_SLOW_DECAY_PCT
    deltas = np.linspace(min_decay, max_decay, C_DIM, dtype=np.float32)
    dl = jnp.asarray(np.tile(deltas, HY_ORDER)[None, :])
    w1 = jnp.pad(lp['hy_ff_w1'], ((0, 64 - HY_EMB), (0, 0)))
    half = ln // rb
    return pl.pallas_call(
        _hy_filter_body, grid=(2 * ln // rb,),
        in_specs=[pl.BlockSpec((rb, 64), lambda i: (i, 0)), pl.BlockSpec((rb, 2), lambda i: (i, 0)),
                  _const_spec((64, HY_FF)), _const_spec((1, HY_FF)), _const_spec((HY_FF, HY_FF)),
                  _const_spec((1, HY_FF)),
                  pl.BlockSpec((HY_FF, HY_ORDER * C_DIM), lambda i: (0, i // half)),
                  _const_spec((1, HY_FF)), _const_spec((1, HY_ORDER * C_DIM))],
        out_specs=[pl.BlockSpec((rb, HY_ORDER * C_DIM), lambda i: (i, 0)),
                   pl.BlockSpec((1, HY_ORDER * C_DIM), lambda i: (0, 0))],
        out_shape=[jax.ShapeDtypeStruct((2 * ln, HY_ORDER * C_DIM), F32),
                   jax.ShapeDtypeStruct((1, HY_ORDER * C_DIM), F32)],
        compiler_params=_cparams(("arbitrary",)), name="hy_filter")(
            jnp.asarray(z), jnp.asarray(tcol), w1, lp['hy_ff_b1'].reshape(1, HY_FF), lp['hy_ff_w2'],
            lp['hy_ff_b2'].reshape(1, HY_FF), lp['hy_ff_w3'], lp['hy_sin_freq'].reshape(1, HY_FF), dl)


def _dft_split(n):
    n2 = 64
    return n // n2, n2


def _dft_tables(n):
    n1, n2 = _dft_split(n)
    k1 = np.arange(n1)[:, None]
    m1 = np.arange(n1)[None, :]
    ang1 = 2.0 * np.pi * (k1 * m1 % n1) / n1
    f1 = np.stack([np.cos(ang1), -np.sin(ang1)], axis=1).reshape(2 * n1, n1)
    f1inv = np.stack([np.cos(ang1), -np.sin(ang1)], axis=2).reshape(n1, 2 * n1) / n
    eye = np.eye(DFT_ROWS)
    kk1 = np.arange(n1)[:, None, None]
    kk2 = np.arange(n2)[None, :, None]
    nn2 = np.arange(n2)[None, None, :]
    ang = 2.0 * np.pi * ((nn2 * (kk1 + n1 * kk2)) % n) / n
    c, s = np.cos(ang), np.sin(ang)
    g = np.concatenate([np.concatenate([c, s], axis=2), np.concatenate([-s, c], axis=2)], axis=1)
    as_bf = lambda a: jnp.asarray(a.astype(np.float32), dtype=BF16)
    return (as_bf(np.kron(f1, eye)), as_bf(np.kron(f1[:, 0:n1 // 2], eye)),
            as_bf(np.kron(f1inv[0:n1 // 2, :], eye)), as_bf(g))


def _hy_outer_body(f_ref, x_ref, o_ref):
    k1, rows, w = x_ref.shape[1:]
    x = x_ref[0].reshape(k1 * rows, w).astype(BF16)
    y = jnp.dot(f_ref[...], x, preferred_element_type=F32)
    o_ref[0] = y.reshape(y.shape[0] // rows, rows, w)


def _hy_outer(fk, x):
    bsz, k1, n2, w = x.shape
    m1 = fk.shape[0] // DFT_ROWS
    return pl.pallas_call(
        _hy_outer_body, grid=(bsz, n2 // DFT_ROWS),
        in_specs=[pl.BlockSpec(fk.shape, lambda bi, j: (0, 0), pipeline_mode=pl.Buffered(1)),
                  pl.BlockSpec((1, k1, DFT_ROWS, w), lambda bi, j: (bi, 0, j, 0))],
        out_specs=pl.BlockSpec((1, m1, DFT_ROWS, w), lambda bi, j: (bi, 0, j, 0)),
        out_shape=jax.ShapeDtypeStruct((bsz, m1, n2, w), F32),
        compiler_params=_cparams(("parallel", "parallel")), name="hy_outer")(fk, x)


def _hy_spec_body(g_ref, a_ref, inv_ref, h_ref):
    kb = g_ref.shape[0]
    for i in range(kb):
        h = jnp.dot(g_ref[i], a_ref[i].astype(BF16), preferred_element_type=F32) * inv_ref[...]
        h_ref[i] = h.astype(h_ref.dtype)


def _hy_spectrum(g, a, inv_norm, *, kb=8):
    n1, m, w = a.shape
    return pl.pallas_call(
        _hy_spec_body, grid=(n1 // kb,),
        in_specs=[pl.BlockSpec((kb, m, m), lambda i: (i, 0, 0)), pl.BlockSpec((kb, m, w), lambda i: (i, 0, 0)),
                  pl.BlockSpec((1, w), lambda i: (0, 0))],
        out_specs=pl.BlockSpec((kb, m, w), lambda i: (i, 0, 0)),
        out_shape=jax.ShapeDtypeStruct((n1, m, w), BF16),
        compiler_params=_cparams(("parallel",)), name="hy_spectrum")(g, a, inv_norm)


def _hy_conv_body(fk_ref, fki_ref, g_ref, h_ref, u_ref, x_ref, d_ref, o_ref, a_ref, *, kb):
    n1h, n2, c = u_ref.shape[1:]
    n1 = 2 * n1h
    blocks = [slice(j * DFT_ROWS, (j + 1) * DFT_ROWS) for j in range(n2 // DFT_ROWS)]
    fk = fk_ref[...]
    for rows in blocks:
        xb = u_ref[0, :, rows, :].reshape(n1h * DFT_ROWS, c).astype(BF16)
        a_ref[:, rows, :] = jnp.dot(fk, xb, preferred_element_type=F32).reshape(2 * n1, DFT_ROWS, c)

    def group(i, carry):
        ks = [i * kb + t for t in range(kb)]
        xs = [jnp.dot(g_ref[k], a_ref[pl.ds(2 * k, 2)].reshape(2 * n2, c).astype(BF16),
                      preferred_element_type=F32) for k in ks]
        zs = []
        for k, x in zip(ks, xs):
            xr, xi = x[0:n2], x[n2:]
            hr, hi = h_ref[k, 0:n2, :].astype(F32), h_ref[k, n2:, :].astype(F32)
            zs.append(jnp.concatenate([xr * hr - xi * hi, xr * hi + xi * hr], axis=0))
        for k, z in zip(ks, zs):
            a_ref[pl.ds(2 * k, 2)] = _dot_tn(g_ref[k], z).reshape(2, n2, c)
        return carry

    lax.fori_loop(0, n1 // kb, group, 0)
    fki = fki_ref[...]
    for rows in blocks:
        ab = a_ref[:, rows, :].reshape(2 * n1 * DFT_ROWS, c).astype(BF16)
        y = jnp.dot(fki, ab, preferred_element_type=F32).reshape(n1h, DFT_ROWS, c)
        o_ref[0, :, rows, :] = x_ref[0, :, rows, :] * (y + u_ref[0, :, rows, :] * d_ref[...])


def _hy_conv(fk_half, fk_inv, g, hspec, u, xg, dskip, order, *, kb=8):
    bsz, n1h, n2, c = u.shape
    n1 = 2 * n1h
    seq = lambda mode: pl.BlockSpec((1, n1h, n2, c), lambda bi: (bi, 0, 0, 0), pipeline_mode=mode)
    body = functools.partial(_hy_conv_body, kb=kb)
    return pl.pallas_call(
        body, grid=(bsz,),
        in_specs=[_const_spec(fk_half.shape), _const_spec(fk_inv.shape), _const_spec(g.shape),
                  pl.BlockSpec((n1, 2 * n2, c), lambda bi: (0, 0, order), pipeline_mode=pl.Buffered(1)),
                  seq(pl.Buffered(1)), seq(pl.Buffered(1)), _const_spec((1, c))],
        out_specs=pl.BlockSpec((1, n1h, n2, c), lambda bi: (bi, 0, 0, 0)),
        out_shape=jax.ShapeDtypeStruct(u.shape, F32),
        scratch_shapes=[pltpu.VMEM((2 * n1, n2, c), F32)],
        compiler_params=pltpu.CompilerParams(dimension_semantics=("parallel",), vmem_limit_bytes=HY_VMEM_LIMIT),
        name="hy_conv")(fk_half, fk_inv, g, hspec, u, xg, dskip)


def _hyena(pc, lp, *, batch, seq_len):
    ln = seq_len
    n = 2 * ln
    n1, n2 = _dft_split(n)
    fk_full, fk_half, fk_inv, g = _dft_tables(n)
    v, x1, x2 = _hy_short(pc, lp, seq_len=ln)
    kern, l1 = _hy_filter(lp, ln)
    ka = _hy_outer(fk_full, kern.reshape(1, n1, n2, HY_ORDER * C_DIM))
    hspec = _hy_spectrum(g, ka.reshape(n1, 2 * n2, HY_ORDER * C_DIM), 1.0 / l1)
    view = lambda t: t.reshape(batch, n1 // 2, n2, C_DIM)

    def conv_gate(u, xg, order):
        return _hy_conv(fk_half, fk_inv, g, hspec, u, xg, lp['hy_bias'][order:order + 1], order)

    z = conv_gate(view(v), view(x1), 0)
    y = conv_gate(z, view(x2), 1)
    return y.reshape(batch * ln, C_DIM)


def _trunk(x, params, final_norm):
    batch, seq_len, _ = x.shape
    x = x.reshape(batch * seq_len, D_MODEL)
    mix = None
    for layer in range(DEPTH):
        lp = {name: arr[layer] for name, arr in params.items()}
        if mix is not None:
            prev = {name: arr[layer - 1] for name, arr in params.items()}
            x = _ffn(x, prev['norm_ffn2'], prev['ffn2_w_in'].astype(BF16), prev['ffn2_w_out'].astype(BF16), mix=mix)
        x = _ffn(x, lp['norm_ffn1'], lp['ffn1_w_in'].astype(BF16), lp['ffn1_w_out'].astype(BF16))
        w = lp['w_in']
        w_pad = jnp.concatenate([w[:, :A_COLS], jnp.zeros((D_MODEL, A_PAD - A_COLS), F32), w[:, A_COLS:]], axis=1)
        pa, pb, pc = _proj(x, lp['norm_mix'], w_pad.astype(BF16))
        prep = _rwkv_prep(pa, lp, seq_len=seq_len)
        yf, yb = _wkv(prep, lp, batch=batch, seq_len=seq_len)
        nb = _natten(pb, lp['na_rpb'], lp['na_out_norm'], batch=batch, seq_len=seq_len)
        yc = _hyena(pc, lp, batch=batch, seq_len=seq_len)
        mix = (yf, yb, prep[3], nb, yc, lp['w_out'].astype(BF16), lp['hy_out_norm'])
    last = {name: arr[DEPTH - 1] for name, arr in params.items()}
    x = _ffn(x, last['norm_ffn2'], last['ffn2_w_in'].astype(BF16), last['ffn2_w_out'].astype(BF16), mix=mix,
             final_g=final_norm)
    return x.reshape(batch, seq_len, D_MODEL)


def kernel(x_prompt, x_sample, norm_ffn1, ffn1_w_in, ffn1_w_out, norm_mix, w_in, rwkv_mu, rwkv_w0, rwkv_w_up, rwkv_a0, rwkv_a_up, rwkv_g_up, rwkv_k_k, rwkv_k_a, rwkv_r_k, rwkv_ln_g, rwkv_ln_b, na_rpb, na_out_norm, hy_conv_w, hy_conv_b, hy_ff_w1, hy_ff_b1, hy_ff_w2, hy_ff_b2, hy_ff_w3, hy_sin_freq, hy_bias, hy_out_norm, w_out, norm_ffn2, ffn2_w_in, ffn2_w_out, final_norm):
    params = {
        'norm_ffn1': norm_ffn1, 'ffn1_w_in': ffn1_w_in, 'ffn1_w_out': ffn1_w_out,
        'norm_mix': norm_mix, 'w_in': w_in,
        'rwkv_mu': rwkv_mu, 'rwkv_w0': rwkv_w0, 'rwkv_w_up': rwkv_w_up, 'rwkv_a0': rwkv_a0,
        'rwkv_a_up': rwkv_a_up, 'rwkv_g_up': rwkv_g_up, 'rwkv_k_k': rwkv_k_k, 'rwkv_k_a': rwkv_k_a,
        'rwkv_r_k': rwkv_r_k.reshape(DEPTH, A_DIM), 'rwkv_ln_g': rwkv_ln_g, 'rwkv_ln_b': rwkv_ln_b,
        'na_rpb': na_rpb, 'na_out_norm': na_out_norm,
        'hy_conv_w': hy_conv_w, 'hy_conv_b': hy_conv_b, 'hy_ff_w1': hy_ff_w1, 'hy_ff_b1': hy_ff_b1,
        'hy_ff_w2': hy_ff_w2, 'hy_ff_b2': hy_ff_b2, 'hy_ff_w3': hy_ff_w3, 'hy_sin_freq': hy_sin_freq,
        'hy_bias': hy_bias, 'hy_out_norm': hy_out_norm,
        'w_out': w_out, 'norm_ffn2': norm_ffn2, 'ffn2_w_in': ffn2_w_in, 'ffn2_w_out': ffn2_w_out,
    }
    return (_trunk(x_prompt, params, final_norm), _trunk(x_sample, params, final_norm))
```

```python
import functools
import math

import numpy as np
import jax
import jax.numpy as jnp
from jax import lax
from jax.experimental import pallas as pl
from jax.experimental.pallas import tpu as pltpu

F32 = jnp.float32
BF16 = jnp.bfloat16

D_MODEL = 1024
DEPTH = 4
GRID_W = 64
HEAD_DIM = 64
A_HEADS = 4
A_DIM = A_HEADS * HEAD_DIM
A_LORA_W = 32
A_LORA_A = 32
A_LORA_G = 64
B_HEADS = 8
B_DIM = B_HEADS * HEAD_DIM
NA_KH_MAX = 8
NA_KW = 16
C_DIM = D_MODEL - A_DIM - B_DIM
C_GROUPS = C_DIM // HEAD_DIM
HY_ORDER = 2
HY_BANDS = 16
HY_EMB = 1 + 2 * HY_BANDS
HY_FF = 64
HY_FAST_DECAY_PCT = 0.3
HY_SLOW_DECAY_PCT = 1.5
HY_TARGET = 1e-2
D_FF = 2816
NORM_EPS = 1e-5
GN_EPS = 64e-5
A_COLS = 3 * A_DIM + 2 * A_LORA_W + 2 * A_LORA_A + A_LORA_G
A_PAD = 1024
B_COLS = 3 * B_DIM
C_COLS = (HY_ORDER + 1) * C_DIM

SUBLANES = 8
VMEM_LIMIT = 56 * 1024 * 1024
HY_VMEM_LIMIT = 60 * 1024 * 1024
WKV_CHUNK = 64
NEG_BIG = -1e30
DFT_ROWS = SUBLANES


def _cparams(sem):
    return pltpu.CompilerParams(dimension_semantics=sem, vmem_limit_bytes=VMEM_LIMIT)


def _const_spec(shape):
    nd = len(shape)
    return pl.BlockSpec(shape, lambda *_: (0,) * nd, pipeline_mode=pl.Buffered(1))


def _dot(a, b):
    return jnp.dot(a.astype(BF16), b.astype(BF16), preferred_element_type=F32)


def _dot_nt(a, b):
    return lax.dot_general(a.astype(BF16), b.astype(BF16), (((1,), (1,)), ((), ())),
                           preferred_element_type=F32)


def _dot_tn(a, b):
    return lax.dot_general(a.astype(BF16), b.astype(BF16), (((0,), (0,)), ((), ())),
                           preferred_element_type=F32)


def _split(a):
    hi = a.astype(BF16)
    lo = (a - hi.astype(F32)).astype(BF16)
    return hi, lo


def _dot_lhs2(a, b_bf16):
    hi, lo = _split(a)
    return (jnp.dot(hi, b_bf16, preferred_element_type=F32)
            + jnp.dot(lo, b_bf16, preferred_element_type=F32))


def _dot3(a, b):
    ah, al = _split(a)
    bh, bl = _split(b)
    return (jnp.dot(ah, bh, preferred_element_type=F32)
            + jnp.dot(al, bh, preferred_element_type=F32)
            + jnp.dot(ah, bl, preferred_element_type=F32))


def _rms(x, g):
    return x * lax.rsqrt(jnp.mean(x * x, axis=-1, keepdims=True) + NORM_EPS) * g


def _group_mat(n, group, value):
    idx = np.arange(n) // group
    return jnp.asarray((idx[:, None] == idx[None, :]).astype(np.float32) * value, dtype=BF16)


def _ffn_body(*refs, n_chunks, with_mix, with_final):
    it = iter(refs)
    x_ref = next(it)
    if with_mix:
        yf_ref, yb_ref, g_ref, nb_ref, yc_ref, wo_ref, cn_ref, pavg_ref = (next(it) for _ in range(8))
    ng_ref, win_ref, wout_ref = next(it), next(it), next(it)
    if with_final:
        fn_ref = next(it)
    o_ref = next(it)

    x = x_ref[...]
    if with_mix:
        ya = (yf_ref[...] + yb_ref[...]) * g_ref[...]
        yc = yc_ref[...]
        ms = _dot_lhs2(yc * yc, pavg_ref[...])
        yc = yc * lax.rsqrt(ms + NORM_EPS) * cn_ref[...]
        x = (x + _dot(ya, wo_ref[0:A_DIM, :]) + _dot(nb_ref[...], wo_ref[A_DIM:A_DIM + B_DIM, :])
             + _dot(yc, wo_ref[A_DIM + B_DIM:, :]))
    h = _rms(x, ng_ref[...]).astype(BF16)
    cw = D_FF // n_chunks
    acc = jnp.zeros_like(x)
    for c in range(n_chunks):
        gate = jnp.dot(h, win_ref[:, c * cw:(c + 1) * cw], preferred_element_type=F32)
        up = jnp.dot(h, win_ref[:, D_FF + c * cw:D_FF + (c + 1) * cw], preferred_element_type=F32)
        act = (gate * jax.nn.sigmoid(gate) * up).astype(BF16)
        acc = acc + jnp.dot(act, wout_ref[c * cw:(c + 1) * cw, :], preferred_element_type=F32)
    y = x + 0.5 * acc
    if with_final:
        y = _rms(y, fn_ref[...])
    o_ref[...] = y


def _ffn(x, norm_g, w_in, w_out, *, mix=None, final_g=None, tm=512, n_chunks=2):
    n = x.shape[0]
    tok = lambda w: pl.BlockSpec((tm, w), lambda i: (i, 0))
    args, specs = [x], [tok(D_MODEL)]
    if mix is not None:
        yf, yb, g, nb, yc, wo, cn = mix
        args += [yf, yb, g, nb, yc, wo, cn.reshape(1, C_DIM), _group_mat(C_DIM, HEAD_DIM, 1.0 / HEAD_DIM)]
        specs += [tok(A_DIM), tok(A_DIM), tok(A_DIM), tok(B_DIM), tok(C_DIM),
                  _const_spec((D_MODEL, D_MODEL)), _const_spec((1, C_DIM)), _const_spec((C_DIM, C_DIM))]
    args += [norm_g.reshape(1, D_MODEL), w_in, w_out]
    specs += [_const_spec((1, D_MODEL)), _const_spec((D_MODEL, 2 * D_FF)), _const_spec((D_FF, D_MODEL))]
    if final_g is not None:
        args.append(final_g.reshape(1, D_MODEL))
        specs.append(_const_spec((1, D_MODEL)))
    body = functools.partial(_ffn_body, n_chunks=n_chunks, with_mix=mix is not None,
                             with_final=final_g is not None)
    return pl.pallas_call(
        body, grid=(n // tm,), in_specs=specs, out_specs=tok(D_MODEL),
        out_shape=jax.ShapeDtypeStruct((n, D_MODEL), F32),
        compiler_params=_cparams(("parallel",)), name="ffn")(*args)


def _halo_specs(tm, width, n_tok):
    per = tm // SUBLANES
    last = n_tok // SUBLANES - 1
    prev = pl.BlockSpec((SUBLANES, width), lambda i: (jnp.maximum(i * per - 1, 0), 0))
    nxt = pl.BlockSpec((SUBLANES, width), lambda i: (jnp.minimum((i + 1) * per, last), 0))
    return prev, nxt


def _shifted(p, row_before, row_after, blocks_per_seq):
    tm = p.shape[0]
    j = pl.program_id(0) % blocks_per_seq
    hp = jnp.where(j == 0, 0.0, row_before)
    hn = jnp.where(j == blocks_per_seq - 1, 0.0, row_after)
    row = lax.broadcasted_iota(jnp.int32, p.shape, 0)
    prev = jnp.where(row == 0, hp, pltpu.roll(p, 1, axis=0))
    nxt = jnp.where(row == tm - 1, hn, pltpu.roll(p, tm - 1, axis=0))
    return prev, nxt


def _mixin_body(x_ref, xp_ref, xn_ref, g_ref, w_ref, mu_ref, wlo_ref, w0a0_ref, gup_ref, kk_ref, ka_ref, rk_ref,
                hsum_ref, cw_ref, cb_ref,
                pb_o, r_o, v_o, kk_o, g_o, lw0_o, lw1_o, kd0_o, kd1_o, b0_o, b1_o, bv0_o, bv1_o, hv_o, hx1_o, hx2_o,
                *, blocks_per_seq):
    a_cols = slice(0, A_PAD)
    c_cols = slice(A_PAD + B_COLS, A_PAD + B_COLS + C_COLS)
    h = _rms(x_ref[...], g_ref[...]).astype(BF16)
    pa = jnp.dot(h, w_ref[:, a_cols], preferred_element_type=F32)
    pb_o[...] = jnp.dot(h, w_ref[:, A_PAD:A_PAD + B_COLS], preferred_element_type=F32).astype(BF16)
    pc = jnp.dot(h, w_ref[:, c_cols], preferred_element_type=F32)
    hh = _rms(jnp.concatenate([xp_ref[...], xn_ref[...]], axis=0), g_ref[...]).astype(BF16)
    ha = jnp.dot(hh, w_ref[:, a_cols], preferred_element_type=F32)
    hc = jnp.dot(hh, w_ref[:, c_cols], preferred_element_type=F32)
    before, after = slice(SUBLANES - 1, SUBLANES), slice(SUBLANES, SUBLANES + 1)

    prev, nxt = _shifted(pc, hc[before], hc[after], blocks_per_seq)
    y = prev * cw_ref[0:1, :] + pc * cw_ref[1:2, :] + nxt * cw_ref[2:3, :] + cb_ref[...]
    hv_o[...] = y[:, 0:C_DIM]
    hx1_o[...] = y[:, C_DIM:2 * C_DIM]
    hx2_o[...] = y[:, 2 * C_DIM:]

    prev, nxt = _shifted(pa, ha[before], ha[after], blocks_per_seq)
    p = pa + mu_ref[...] * (0.5 * (prev + nxt) - pa)
    r = p[:, 0:A_DIM]
    k = p[:, A_DIM:2 * A_DIM]
    v = p[:, 2 * A_DIM:3 * A_DIM]
    lo = p[:, 3 * A_DIM:3 * A_DIM + 128]
    gd = p[:, 3 * A_DIM + 128:3 * A_DIM + 256]
    lane = lax.broadcasted_iota(jnp.int32, lo.shape, 1)
    lo = jnp.where(lane < 2 * A_LORA_W, jnp.tanh(lo), lo)
    wa = _dot3(lo, wlo_ref[...]) + w0a0_ref[...]
    g = _dot3(jax.nn.sigmoid(gd), gup_ref[...])
    hsum = hsum_ref[...]
    kkv = k * kk_ref[...]
    nrm = jnp.sqrt(_dot_lhs2(kkv * kkv, hsum))
    kkv = kkv / jnp.maximum(nrm, 1e-12)
    r_o[...] = r
    v_o[...] = v
    kk_o[...] = kkv
    g_o[...] = g
    for d, (lw_o, kd_o, b_o, bv_o) in enumerate(((lw0_o, kd0_o, b0_o, bv0_o), (lw1_o, kd1_o, b1_o, bv1_o))):
        w_log = -jax.nn.softplus(-wa[:, d * A_DIM:(d + 1) * A_DIM]) - 0.5
        a = jax.nn.sigmoid(wa[:, (2 + d) * A_DIM:(3 + d) * A_DIM])
        kd = k * (1.0 + (a - 1.0) * ka_ref[...])
        lw_o[...] = -jnp.exp(w_log)
        kd_o[...] = kd
        b_o[...] = kkv * a
        bv_o[...] = _dot_lhs2(r * kd * rk_ref[...], hsum) * v


def _mixin(x, lp, *, seq_len, tm=512):
    n = x.shape[0]
    tok = lambda w: pl.BlockSpec((tm, w), lambda i: (i, 0))
    prev, nxt = _halo_specs(tm, D_MODEL, n)
    w = lp['w_in']
    w_pad = jnp.concatenate([w[:, :A_COLS], jnp.zeros((D_MODEL, A_PAD - A_COLS), F32), w[:, A_COLS:]],
                            axis=1).astype(BF16)
    mu = jnp.pad(lp['rwkv_mu'], (0, A_PAD - A_COLS)).reshape(1, A_PAD)
    wlo = jnp.zeros((128, 4 * A_DIM), F32)
    for d in range(2):
        wlo = wlo.at[d * A_LORA_W:(d + 1) * A_LORA_W, d * A_DIM:(d + 1) * A_DIM].set(lp['rwkv_w_up'][d])
        wlo = wlo.at[64 + d * A_LORA_A:64 + (d + 1) * A_LORA_A, (2 + d) * A_DIM:(3 + d) * A_DIM].set(lp['rwkv_a_up'][d])
    w0a0 = jnp.concatenate([lp['rwkv_w0'].reshape(-1), lp['rwkv_a0'].reshape(-1)]).reshape(1, 4 * A_DIM)
    gup = jnp.pad(lp['rwkv_g_up'], ((0, 128 - A_LORA_G), (0, 0)))
    consts = [lp['norm_mix'].reshape(1, D_MODEL), w_pad, mu, wlo, w0a0, gup, lp['rwkv_k_k'].reshape(1, A_DIM),
              lp['rwkv_k_a'].reshape(1, A_DIM), lp['rwkv_r_k'].reshape(1, A_DIM), _group_mat(A_DIM, HEAD_DIM, 1.0),
              jnp.pad(lp['hy_conv_w'], ((0, SUBLANES - 3), (0, 0))), lp['hy_conv_b'].reshape(1, C_COLS)]
    body = functools.partial(_mixin_body, blocks_per_seq=seq_len // tm)
    outs = pl.pallas_call(
        body, grid=(n // tm,),
        in_specs=[tok(D_MODEL), prev, nxt] + [_const_spec(c.shape) for c in consts],
        out_specs=[tok(B_COLS)] + [tok(A_DIM)] * 12 + [tok(C_DIM)] * 3,
        out_shape=[jax.ShapeDtypeStruct((n, B_COLS), BF16)] + [jax.ShapeDtypeStruct((n, A_DIM), F32)] * 15,
        compiler_params=_cparams(("parallel",)), name="mixin")(x, x, x, *consts)
    return outs[0], outs[1:13], outs[13:16]


def _wkv_setup(r, v, kk, lw, kd, b, reverse):
    tb = r.shape[0]
    c_len = WKV_CHUNK
    pos = lax.broadcasted_iota(jnp.int32, (tb, A_DIM), 0) % c_len
    cum = lw
    sh = 1
    while sh < c_len:
        if reverse:
            cum = cum + jnp.where(pos < c_len - sh, pltpu.roll(cum, tb - sh, axis=0), 0.0)
        else:
            cum = cum + jnp.where(pos >= sh, pltpu.roll(cum, sh, axis=0), 0.0)
        sh *= 2
    e_neg = jnp.exp(-cum)
    return dict(cum=cum, rq=r * jnp.exp(cum), kq=kk * jnp.exp(cum - lw), kbar=kd * e_neg, bbar=b * e_neg, v=v)


def _wkv_body(rf, vf, kkf, lwf, kdf, bf, bvf, rb, vb, kkb, lwb, kdb, bb, bvb, lng_ref, lnb_ref, pavg_ref,
              of_ref, ob_ref, sf_ref, sb_ref):
    @pl.when(pl.program_id(1) == 0)
    def _():
        sf_ref[...] = jnp.zeros_like(sf_ref)
        sb_ref[...] = jnp.zeros_like(sb_ref)

    c_len = WKV_CHUNK
    tb = rf.shape[0]
    nc = tb // c_len
    dirs = (_wkv_setup(rf[...], vf[...], kkf[...], lwf[...], kdf[...], bf[...], False),
            _wkv_setup(rb[...], vb[...], kkb[...], lwb[...], kdb[...], bb[...], True))
    s_refs, o_refs, bvs = (sf_ref, sb_ref), (of_ref, ob_ref), (bvf, bvb)

    t_idx = lax.broadcasted_iota(jnp.int32, (c_len, A_DIM), 0)
    s_idx = lax.broadcasted_iota(jnp.int32, (c_len, A_DIM), 1) % c_len
    incl = (s_idx <= t_idx, s_idx >= t_idx)
    strict = (s_idx < t_idx, s_idx > t_idx)
    eye = (s_idx == t_idx).astype(F32)
    bd_mask = (lax.broadcasted_iota(jnp.int32, (A_DIM, A_DIM), 0) // HEAD_DIM
               == lax.broadcasted_iota(jnp.int32, (A_DIM, A_DIM), 1) // HEAD_DIM)
    zero_bf = jnp.zeros((A_DIM, A_DIM), BF16)

    def stack_hl(x_hl):
        return tuple(jnp.where(bd_mask, jnp.concatenate([part] * A_HEADS, axis=0), zero_bf) for part in x_hl)

    def stack(x):
        return stack_hl(_split(x))

    def mm_hl(a_hl, b_hl, nt=False):
        bh, bl = b_hl
        ah, al = a_hl
        m = ah.shape[0]
        dg = _dot_nt if nt else _dot
        p = dg(jnp.concatenate([ah, al], axis=0), bh)
        return p[0:m] + p[m:] + dg(ah, bl)

    def mm(a, b_hl, nt=False):
        return mm_hl(_split(a), b_hl, nt)

    items = [(d, c) for c in range(nc) for d in (0, 1)]
    ch = {}
    for d, c in items:
        sl = slice(c * c_len, (c + 1) * c_len)
        q = dirs[d]
        ch[d, c] = dict(rq=q['rq'][sl], kq=q['kq'][sl], kbar=q['kbar'][sl], bbar=q['bbar'][sl], v=q['v'][sl])
    for key in items:
        e = ch[key]
        kst, bst = stack(e['kbar']), stack(e['bbar'])
        acat = mm(jnp.concatenate([e['rq'], e['kq']], axis=0),
                  (jnp.concatenate([kst[0], bst[0]], axis=0), jnp.concatenate([kst[1], bst[1]], axis=0)),
                  nt=True)
        d = key[0]
        e['a_rv'] = jnp.concatenate([jnp.where(incl[d], acat[0:c_len, 0:A_DIM], 0.0),
                                     jnp.where(strict[d], acat[c_len:, 0:A_DIM], 0.0)], axis=0)
        e['a_rb'] = jnp.where(incl[d], acat[0:c_len, A_DIM:], 0.0)
        e['a_kb'] = jnp.where(strict[d], acat[c_len:, A_DIM:], 0.0)
        e['a_kb_hl'] = _split(e['a_kb'])
    half = 1
    zero_c = jnp.zeros((c_len, A_DIM), BF16)
    while half < c_len:
        in_block = (t_idx // (2 * half)) == (s_idx // (2 * half))
        lo_t, lo_s = t_idx % (2 * half) < half, s_idx % (2 * half) < half
        off = (in_block & ~lo_t & lo_s, in_block & lo_t & ~lo_s)
        if half == 1:
            for key in items:
                ch[key]['tinv'] = eye - jnp.where(off[key[0]], ch[key]['a_kb'], 0.0)
        else:
            for key in items:
                e = ch[key]
                e['tinv_hl'] = _split(e['tinv'])
                a_off_hl = tuple(jnp.where(off[key[0]], part, zero_c) for part in e['a_kb_hl'])
                e['tmp'] = mm_hl(e['tinv_hl'], stack_hl(a_off_hl))
            for key in items:
                e = ch[key]
                e['tinv'] = e['tinv'] - mm(e['tmp'], stack_hl(e['tinv_hl']))
        half *= 2
    for key in items:
        e = ch[key]
        av = mm(e['a_rv'], stack(e['v']))
        e['o_intra'], e['akkv'] = av[0:c_len], av[c_len:]
    for key in items:
        e = ch[key]
        t_hl = _split(e['tinv'])
        e['w_t'] = mm_hl(t_hl, stack(e['kq']))
        e['u0'] = mm_hl(t_hl, stack(e['akkv']))
    outs = ([None] * nc, [None] * nc)
    for step in range(nc):
        cur = [(0, step), (1, nc - 1 - step)]
        xs = {}
        for d, c in cur:
            e = ch[d, c]
            e['s_old'] = s_refs[d][...]
            xs[d] = mm(jnp.concatenate([e['rq'], e['w_t']], axis=0), _split(e['s_old']), nt=True)
        for d, c in cur:
            e = ch[d, c]
            u = -(xs[d][c_len:] + e['u0'])
            edge = c * c_len if d else (c + 1) * c_len - 1
            e_end = jnp.exp(dirs[d]['cum'][edge:edge + 1, :])
            vu_h, vu_l = _split(jnp.concatenate([e['v'], u], axis=0))
            kb_h, kb_l = _split(jnp.concatenate([e['kbar'] * e_end, e['bbar'] * e_end], axis=0))
            ds = _dot_tn(jnp.concatenate([vu_h, vu_l, vu_h], axis=0), jnp.concatenate([kb_h, kb_h, kb_l], axis=0))
            s_refs[d][...] = e['s_old'] * e_end + jnp.where(bd_mask, ds, 0.0)
            outs[d][c] = xs[d][0:c_len] + e['o_intra'] + mm(e['a_rb'], stack(u))
    lng, lnb, pavg = lng_ref[...], lnb_ref[...], pavg_ref[...]
    for d in (0, 1):
        o = jnp.concatenate(outs[d], axis=0)
        mu = _dot_lhs2(o, pavg)
        dev = o - mu
        var = _dot_lhs2(dev * dev, pavg)
        o_refs[d][...] = dev * lax.rsqrt(var + GN_EPS) * lng + lnb + bvs[d][...]


def _wkv(prep, lp, *, batch, seq_len, tb=512):
    r, v, kk, _, lw0, lw1, kd0, kd1, b0, b1, bv0, bv1 = prep
    n = r.shape[0]
    nblk = seq_len // tb
    fwd = pl.BlockSpec((tb, A_DIM), lambda bi, j: (bi * nblk + j, 0))
    bwd = pl.BlockSpec((tb, A_DIM), lambda bi, j: (bi * nblk + nblk - 1 - j, 0))
    consts = [lp['rwkv_ln_g'].reshape(1, A_DIM), lp['rwkv_ln_b'].reshape(1, A_DIM),
              _group_mat(A_DIM, HEAD_DIM, 1.0 / HEAD_DIM)]
    cspec = lambda shape: pl.BlockSpec(shape, lambda bi, j: (0, 0), pipeline_mode=pl.Buffered(1))
    return pl.pallas_call(
        _wkv_body, grid=(batch, nblk),
        in_specs=[fwd] * 7 + [bwd] * 7 + [cspec(c.shape) for c in consts],
        out_specs=[fwd, bwd],
        out_shape=[jax.ShapeDtypeStruct((n, A_DIM), F32)] * 2,
        scratch_shapes=[pltpu.VMEM((A_DIM, A_DIM), F32), pltpu.VMEM((A_DIM, A_DIM), F32)],
        compiler_params=_cparams(("parallel", "arbitrary")), name="wkv")(
            r, v, kk, lw0, kd0, b0, bv0, r, v, kk, lw1, kd1, b1, bv1, *consts)


def _na_tables(rpb, kh):
    col = np.arange(GRID_W)
    col_start = np.clip(col - NA_KW // 2, 0, GRID_W - NA_KW)
    col_ok = (col[None, :] >= col_start[:, None]) & (col[None, :] < col_start[:, None] + NA_KW)
    dx_idx = np.clip(col[None, :] - col[:, None] + NA_KW - 1, 0, 2 * NA_KW - 2)
    onehot = ((np.arange(2 * NA_KW - 1)[:, None, None] == dx_idx[None]) & col_ok[None]).astype(np.float32)
    mask_add = np.where(col_ok, 0.0, NEG_BIG).astype(np.float32)
    rpbx = jnp.einsum('hdx,xqk->hdqk', rpb.astype(F32), jnp.asarray(onehot),
                      precision=lax.Precision.HIGHEST) + jnp.asarray(mask_add)
    tabs = []
    for delta in range(kh):
        lo = NA_KH_MAX - 1 - delta
        tab = jnp.transpose(rpbx[:, lo:lo + kh], (0, 2, 1, 3))
        tabs.append(tab.reshape(B_HEADS * GRID_W, kh * GRID_W))
    return jnp.stack(tabs)


def _na_body(*refs, rows, kh, rq):
    q_ref, k_ref, v_ref = refs[0:3]
    bm_refs = refs[3:3 + rq]
    gn_ref, pavg_ref, o_ref = refs[3 + rq:]
    n_pairs = B_HEADS // 2
    pw = 2 * HEAD_DIM
    second = lax.broadcasted_iota(jnp.int32, (GRID_W, pw), 1) >= HEAD_DIM
    wins, ss = [], []
    for i in range(rq):
        r = pl.program_id(1) * rq + i
        rs = jnp.clip(r - kh // 2, 0, rows - kh)
        start = pl.multiple_of(rs * GRID_W, GRID_W)
        wins.append(start)
        parts = []
        for p in range(n_pairs):
            qp = q_ref[0, i * GRID_W:(i + 1) * GRID_W, p * pw:(p + 1) * pw]
            zero = jnp.zeros_like(qp)
            qs = jnp.concatenate([jnp.where(second, zero, qp), jnp.where(second, qp, zero)], axis=0)
            parts.append(_dot_nt(qs, k_ref[0, pl.ds(start, kh * GRID_W), p * pw:(p + 1) * pw]))
        ss.append(jnp.concatenate(parts, axis=0))
    ps, ls = [], []
    for i in range(rq):
        s = ss[i] * (HEAD_DIM ** -0.5) + bm_refs[i][0]
        m = jnp.max(s, axis=-1, keepdims=True)
        p = jnp.exp(s - m)
        ls.append(jnp.sum(p, axis=-1, keepdims=True))
        ps.append(p.astype(BF16))
    outs = []
    for i in range(rq):
        cols = []
        for p in range(n_pairs):
            rows_p = slice(p * 2 * GRID_W, (p + 1) * 2 * GRID_W)
            pv = _dot(ps[i][rows_p], v_ref[0, pl.ds(wins[i], kh * GRID_W), p * pw:(p + 1) * pw]) / ls[i][rows_p]
            cols.append(jnp.where(second, pv[GRID_W:], pv[0:GRID_W]))
        outs.append(jnp.concatenate(cols, axis=1))
    out = jnp.concatenate(outs, axis=0)
    ms = _dot_lhs2(out * out, pavg_ref[...])
    o_ref[0] = (out * lax.rsqrt(ms + NORM_EPS) * gn_ref[...]).astype(o_ref.dtype)


def _natten(pb, rpb, out_norm, *, batch, seq_len, rq=4):
    rows = seq_len // GRID_W
    kh = min(NA_KH_MAX, rows)
    tables = _na_tables(rpb, kh)
    pb3 = pb.reshape(batch, seq_len, B_COLS)

    def delta_map(i):
        def index(bi, j):
            r = j * rq + i
            return (r - jnp.clip(r - kh // 2, 0, rows - kh), 0, 0)
        return index

    body = functools.partial(_na_body, rows=rows, kh=kh, rq=rq)
    out = pl.pallas_call(
        body, grid=(batch, rows // rq),
        in_specs=[pl.BlockSpec((1, rq * GRID_W, B_DIM), lambda bi, j: (bi, j, 0)),
                  pl.BlockSpec((1, seq_len, B_DIM), lambda bi, j: (bi, 0, 1)),
                  pl.BlockSpec((1, seq_len, B_DIM), lambda bi, j: (bi, 0, 2))]
                 + [pl.BlockSpec((1, B_HEADS * GRID_W, kh * GRID_W), delta_map(i)) for i in range(rq)]
                 + [pl.BlockSpec((1, B_DIM), lambda bi, j: (0, 0)),
                    pl.BlockSpec((B_DIM, B_DIM), lambda bi, j: (0, 0))],
        out_specs=pl.BlockSpec((1, rq * GRID_W, B_DIM), lambda bi, j: (bi, j, 0)),
        out_shape=jax.ShapeDtypeStruct((batch, seq_len, B_DIM), BF16),
        compiler_params=_cparams(("parallel", "arbitrary")), name="natten")(
            pb3, pb3, pb3, *([tables] * rq), out_norm.reshape(1, B_DIM),
            _group_mat(B_DIM, HEAD_DIM, 1.0 / HEAD_DIM))
    return out.reshape(batch * seq_len, B_DIM)


def _hy_filter_body(z_ref, t_ref, w1_ref, b1_ref, w2_ref, b2_ref, w3_ref, fr_ref, dl_ref, k_o, s_o):
    fr = fr_ref[...]
    h = jnp.sin(fr * (_dot3(z_ref[...], w1_ref[...]) + b1_ref[...]))
    h = jnp.sin(fr * (_dot3(h, w2_ref[...]) + b2_ref[...]))
    h = _dot3(h, w3_ref[...])
    tcol = t_ref[...]
    kern = h * jnp.exp(-tcol[:, 0:1] * jnp.abs(dl_ref[...])) * tcol[:, 1:2]
    k_o[...] = kern

    @pl.when(pl.program_id(0) == 0)
    def _():
        s_o[...] = jnp.zeros_like(s_o)

    s_o[...] += jnp.sum(jnp.abs(kern), axis=0, keepdims=True)


def _hy_filter(lp, seq_len, *, rb=512):
    ln = seq_len
    pos = np.concatenate([np.arange(ln), np.array([0]), np.arange(ln - 1, 0, -1)]).astype(np.float64)
    valid = np.ones(2 * ln)
    valid[ln] = 0.0
    t32 = np.linspace(0.0, 1.0, ln, dtype=np.float32)[:, None]
    w32 = (np.float32(2.0 * math.pi) * np.arange(ln, dtype=np.float32)[:, None] / np.float32(ln)).astype(np.float32)
    f32 = np.linspace(1e-4, HY_BANDS - 1, HY_BANDS, dtype=np.float32)[None, :]
    z32 = np.concatenate([t32, np.cos(f32 * w32), -np.sin(f32 * w32)], axis=-1).astype(np.float32)
    idx = pos.astype(np.int64)
    z = np.zeros((2 * ln, 64), np.float32)
    z[:, :HY_EMB] = z32[idx]
    tcol = np.stack([t32[idx, 0], valid.astype(np.float32)], axis=-1)
    max_decay = math.log(HY_TARGET) / HY_FAST_DECAY_PCT
    min_decay = math.log(HY_TARGET) / HY_SLOW_DECAY_PCT
    deltas = np.linspace(min_decay, max_decay, C_DIM, dtype=np.float32)
    dl = jnp.asarray(np.tile(deltas, HY_ORDER)[None, :])
    w1 = jnp.pad(lp['hy_ff_w1'], ((0, 64 - HY_EMB), (0, 0)))
    half = ln // rb
    return pl.pallas_call(
        _hy_filter_body, grid=(2 * ln // rb,),
        in_specs=[pl.BlockSpec((rb, 64), lambda i: (i, 0)), pl.BlockSpec((rb, 2), lambda i: (i, 0)),
                  _const_spec((64, HY_FF)), _const_spec((1, HY_FF)), _const_spec((HY_FF, HY_FF)),
                  _const_spec((1, HY_FF)),
                  pl.BlockSpec((HY_FF, HY_ORDER * C_DIM), lambda i: (0, i // half)),
                  _const_spec((1, HY_FF)), _const_spec((1, HY_ORDER * C_DIM))],
        out_specs=[pl.BlockSpec((rb, HY_ORDER * C_DIM), lambda i: (i, 0)),
                   pl.BlockSpec((1, HY_ORDER * C_DIM), lambda i: (0, 0))],
        out_shape=[jax.ShapeDtypeStruct((2 * ln, HY_ORDER * C_DIM), F32),
                   jax.ShapeDtypeStruct((1, HY_ORDER * C_DIM), F32)],
        compiler_params=_cparams(("arbitrary",)), name="hy_filter")(
            jnp.asarray(z), jnp.asarray(tcol), w1, lp['hy_ff_b1'].reshape(1, HY_FF), lp['hy_ff_w2'],
            lp['hy_ff_b2'].reshape(1, HY_FF), lp['hy_ff_w3'], lp['hy_sin_freq'].reshape(1, HY_FF), dl)


def _dft_split(n):
    n2 = 64
    return n // n2, n2


def _dft_tables(n):
    n1, n2 = _dft_split(n)
    k1 = np.arange(n1)[:, None]
    m1 = np.arange(n1)[None, :]
    ang1 = 2.0 * np.pi * (k1 * m1 % n1) / n1
    f1 = np.stack([np.cos(ang1), -np.sin(ang1)], axis=1).reshape(2 * n1, n1)
    f1inv = np.stack([np.cos(ang1), -np.sin(ang1)], axis=2).reshape(n1, 2 * n1) / n
    eye = np.eye(DFT_ROWS)
    kk1 = np.arange(n1)[:, None, None]
    kk2 = np.arange(n2)[None, :, None]
    nn2 = np.arange(n2)[None, None, :]
    ang = 2.0 * np.pi * ((nn2 * (kk1 + n1 * kk2)) % n) / n
    c, s = np.cos(ang), np.sin(ang)
    g = np.concatenate([np.concatenate([c, s], axis=2), np.concatenate([-s, c], axis=2)], axis=1)
    as_bf = lambda a: jnp.asarray(a.astype(np.float32), dtype=BF16)
    return (as_bf(np.kron(f1, eye)), as_bf(np.kron(f1[:, 0:n1 // 2], eye)),
            as_bf(np.kron(f1inv[0:n1 // 2, :], eye)), as_bf(g))


def _hy_outer_body(f_ref, x_ref, o_ref):
    k1, rows, w = x_ref.shape[1:]
    x = x_ref[0].reshape(k1 * rows, w).astype(BF16)
    y = jnp.dot(f_ref[...], x, preferred_element_type=F32)
    o_ref[0] = y.reshape(y.shape[0] // rows, rows, w)


def _hy_outer(fk, x):
    bsz, k1, n2, w = x.shape
    m1 = fk.shape[0] // DFT_ROWS
    return pl.pallas_call(
        _hy_outer_body, grid=(bsz, n2 // DFT_ROWS),
        in_specs=[pl.BlockSpec(fk.shape, lambda bi, j: (0, 0), pipeline_mode=pl.Buffered(1)),
                  pl.BlockSpec((1, k1, DFT_ROWS, w), lambda bi, j: (bi, 0, j, 0))],
        out_specs=pl.BlockSpec((1, m1, DFT_ROWS, w), lambda bi, j: (bi, 0, j, 0)),
        out_shape=jax.ShapeDtypeStruct((bsz, m1, n2, w), F32),
        compiler_params=_cparams(("parallel", "parallel")), name="hy_outer")(fk, x)


def _hy_spec_body(g_ref, a_ref, inv_ref, h_ref):
    kb = g_ref.shape[0]
    for i in range(kb):
        h = jnp.dot(g_ref[i], a_ref[i].astype(BF16), preferred_element_type=F32) * inv_ref[...]
        h_ref[i] = h.astype(h_ref.dtype)


def _hy_spectrum(g, a, inv_norm, *, kb=8):
    n1, m, w = a.shape
    return pl.pallas_call(
        _hy_spec_body, grid=(n1 // kb,),
        in_specs=[pl.BlockSpec((kb, m, m), lambda i: (i, 0, 0)), pl.BlockSpec((kb, m, w), lambda i: (i, 0, 0)),
                  pl.BlockSpec((1, w), lambda i: (0, 0))],
        out_specs=pl.BlockSpec((kb, m, w), lambda i: (i, 0, 0)),
        out_shape=jax.ShapeDtypeStruct((n1, m, w), BF16),
        compiler_params=_cparams(("parallel",)), name="hy_spectrum")(g, a, inv_norm)


def _hy_conv_body(fk_ref, fki_ref, g_ref, h_ref, u_ref, x_ref, d_ref, o_ref, a_ref, *, kb):
    n1h, n2, c = u_ref.shape[1:]
    n1 = 2 * n1h
    blocks = [slice(j * DFT_ROWS, (j + 1) * DFT_ROWS) for j in range(n2 // DFT_ROWS)]
    fk = fk_ref[...]
    for rows in blocks:
        xb = u_ref[0, :, rows, :].reshape(n1h * DFT_ROWS, c).astype(BF16)
        a_ref[:, rows, :] = jnp.dot(fk, xb, preferred_element_type=F32).reshape(2 * n1, DFT_ROWS, c)

    def group(i, carry):
        ks = [i * kb + t for t in range(kb)]
        xs = [jnp.dot(g_ref[k], a_ref[pl.ds(2 * k, 2)].reshape(2 * n2, c).astype(BF16),
                      preferred_element_type=F32) for k in ks]
        zs = []
        for k, x in zip(ks, xs):
            xr, xi = x[0:n2], x[n2:]
            hr, hi = h_ref[k, 0:n2, :].astype(F32), h_ref[k, n2:, :].astype(F32)
            zs.append(jnp.concatenate([xr * hr - xi * hi, xr * hi + xi * hr], axis=0))
        for k, z in zip(ks, zs):
            a_ref[pl.ds(2 * k, 2)] = _dot_tn(g_ref[k], z).reshape(2, n2, c)
        return carry

    lax.fori_loop(0, n1 // kb, group, 0)
    fki = fki_ref[...]
    for rows in blocks:
        ab = a_ref[:, rows, :].reshape(2 * n1 * DFT_ROWS, c).astype(BF16)
        y = jnp.dot(fki, ab, preferred_element_type=F32).reshape(n1h, DFT_ROWS, c)
        o_ref[0, :, rows, :] = x_ref[0, :, rows, :] * (y + u_ref[0, :, rows, :] * d_ref[...])


def _hy_conv(fk_half, fk_inv, g, hspec, u, xg, dskip, order, *, kb=8):
    bsz, n1h, n2, c = u.shape
    n1 = 2 * n1h
    seq = lambda mode: pl.BlockSpec((1, n1h, n2, c), lambda bi: (bi, 0, 0, 0), pipeline_mode=mode)
    body = functools.partial(_hy_conv_body, kb=kb)
    return pl.pallas_call(
        body, grid=(bsz,),
        in_specs=[_const_spec(fk_half.shape), _const_spec(fk_inv.shape), _const_spec(g.shape),
                  pl.BlockSpec((n1, 2 * n2, c), lambda bi: (0, 0, order), pipeline_mode=pl.Buffered(1)),
                  seq(pl.Buffered(1)), seq(pl.Buffered(1)), _const_spec((1, c))],
        out_specs=pl.BlockSpec((1, n1h, n2, c), lambda bi: (bi, 0, 0, 0)),
        out_shape=jax.ShapeDtypeStruct(u.shape, F32),
        scratch_shapes=[pltpu.VMEM((2 * n1, n2, c), F32)],
        compiler_params=pltpu.CompilerParams(dimension_semantics=("parallel",), vmem_limit_bytes=HY_VMEM_LIMIT),
        name="hy_conv")(fk_half, fk_inv, g, hspec, u, xg, dskip)


def _hyena(vxx, lp, *, batch, seq_len):
    ln = seq_len
    n = 2 * ln
    n1, n2 = _dft_split(n)
    fk_full, fk_half, fk_inv, g = _dft_tables(n)
    v, x1, x2 = vxx
    kern, l1 = _hy_filter(lp, ln)
    ka = _hy_outer(fk_full, kern.reshape(1, n1, n2, HY_ORDER * C_DIM))
    hspec = _hy_spectrum(g, ka.reshape(n1, 2 * n2, HY_ORDER * C_DIM), 1.0 / l1)
    view = lambda t: t.reshape(batch, n1 // 2, n2, C_DIM)

    def conv_gate(u, xg, order):
        return _hy_conv(fk_half, fk_inv, g, hspec, u, xg, lp['hy_bias'][order:order + 1], order)

    z = conv_gate(view(v), view(x1), 0)
    y = conv_gate(z, view(x2), 1)
    return y.reshape(batch * ln, C_DIM)


def _trunk(x, params, final_norm):
    batch, seq_len, _ = x.shape
    x = x.reshape(batch * seq_len, D_MODEL)
    mix = None
    for layer in range(DEPTH):
        lp = {name: arr[layer] for name, arr in params.items()}
        if mix is not None:
            prev = {name: arr[layer - 1] for name, arr in params.items()}
            x = _ffn(x, prev['norm_ffn2'], prev['ffn2_w_in'].astype(BF16), prev['ffn2_w_out'].astype(BF16), mix=mix)
        x = _ffn(x, lp['norm_ffn1'], lp['ffn1_w_in'].astype(BF16), lp['ffn1_w_out'].astype(BF16))
        pb, prep, vxx = _mixin(x, lp, seq_len=seq_len)
        yf, yb = _wkv(prep, lp, batch=batch, seq_len=seq_len)
        nb = _natten(pb, lp['na_rpb'], lp['na_out_norm'], batch=batch, seq_len=seq_len)
        yc = _hyena(vxx, lp, batch=batch, seq_len=seq_len)
        mix = (yf, yb, prep[3], nb, yc, lp['w_out'].astype(BF16), lp['hy_out_norm'])
    last = {name: arr[DEPTH - 1] for name, arr in params.items()}
    x = _ffn(x, last['norm_ffn2'], last['ffn2_w_in'].astype(BF16), last['ffn2_w_out'].astype(BF16), mix=mix,
             final_g=final_norm)
    return x.reshape(batch, seq_len, D_MODEL)


def kernel(x_prompt, x_sample, norm_ffn1, ffn1_w_in, ffn1_w_out, norm_mix, w_in, rwkv_mu, rwkv_w0, rwkv_w_up, rwkv_a0, rwkv_a_up, rwkv_g_up, rwkv_k_k, rwkv_k_a, rwkv_r_k, rwkv_ln_g, rwkv_ln_b, na_rpb, na_out_norm, hy_conv_w, hy_conv_b, hy_ff_w1, hy_ff_b1, hy_ff_w2, hy_ff_b2, hy_ff_w3, hy_sin_freq, hy_bias, hy_out_norm, w_out, norm_ffn2, ffn2_w_in, ffn2_w_out, final_norm):
    params = {
        'norm_ffn1': norm_ffn1, 'ffn1_w_in': ffn1_w_in, 'ffn1_w_out': ffn1_w_out,
        'norm_mix': norm_mix, 'w_in': w_in,
        'rwkv_mu': rwkv_mu, 'rwkv_w0': rwkv_w0, 'rwkv_w_up': rwkv_w_up, 'rwkv_a0': rwkv_a0,
        'rwkv_a_up': rwkv_a_up, 'rwkv_g_up': rwkv_g_up, 'rwkv_k_k': rwkv_k_k, 'rwkv_k_a': rwkv_k_a,
        'rwkv_r_k': rwkv_r_k.reshape(DEPTH, A_DIM), 'rwkv_ln_g': rwkv_ln_g, 'rwkv_ln_b': rwkv_ln_b,
        'na_rpb': na_rpb, 'na_out_norm': na_out_norm,
        'hy_conv_w': hy_conv_w, 'hy_conv_b': hy_conv_b, 'hy_ff_w1': hy_ff_w1, 'hy_ff_b1': hy_ff_b1,
        'hy_ff_w2': hy_ff_w2, 'hy_ff_b2': hy_ff_b2, 'hy_ff_w3': hy_ff_w3, 'hy_sin_freq': hy_sin_freq,
        'hy_bias': hy_bias, 'hy_out_norm': hy_out_norm,
        'w_out': w_out, 'norm_ffn2': norm_ffn2, 'ffn2_w_in': ffn2_w_in, 'ffn2_w_out': ffn2_w_out,
    }
    return (_trunk(x_prompt, params, final_norm), _trunk(x_sample, params, final_norm))
```

```python
import functools
import math

import numpy as np
import jax
import jax.numpy as jnp
from jax import lax
from jax.experimental import pallas as pl
from jax.experimental.pallas import tpu as pltpu

F32 = jnp.float32
BF16 = jnp.bfloat16

D_MODEL = 1024
DEPTH = 4
GRID_W = 64
HEAD_DIM = 64
A_HEADS = 4
A_DIM = A_HEADS * HEAD_DIM
A_LORA_W = 32
A_LORA_A = 32
A_LORA_G = 64
B_HEADS = 8
B_DIM = B_HEADS * HEAD_DIM
NA_KH_MAX = 8
NA_KW = 16
C_DIM = D_MODEL - A_DIM - B_DIM
C_GROUPS = C_DIM // HEAD_DIM
HY_ORDER = 2
HY_BANDS = 16
HY_EMB = 1 + 2 * HY_BANDS
HY_FF = 64
HY_FAST_DECAY_PCT = 0.3
HY_SLOW_DECAY_PCT = 1.5
HY_TARGET = 1e-2
D_FF = 2816
NORM_EPS = 1e-5
GN_EPS = 64e-5
A_COLS = 3 * A_DIM + 2 * A_LORA_W + 2 * A_LORA_A + A_LORA_G
A_PAD = 1024
B_COLS = 3 * B_DIM
C_COLS = (HY_ORDER + 1) * C_DIM

SUBLANES = 8
VMEM_LIMIT = 56 * 1024 * 1024
HY_VMEM_LIMIT = 60 * 1024 * 1024
WKV_CHUNK = 64
NEG_BIG = -1e30
DFT_ROWS = SUBLANES


def _cparams(sem):
    return pltpu.CompilerParams(dimension_semantics=sem, vmem_limit_bytes=VMEM_LIMIT)


def _const_spec(shape):
    nd = len(shape)
    return pl.BlockSpec(shape, lambda *_: (0,) * nd, pipeline_mode=pl.Buffered(1))


def _dot(a, b):
    return jnp.dot(a.astype(BF16), b.astype(BF16), preferred_element_type=F32)


def _dot_nt(a, b):
    return lax.dot_general(a.astype(BF16), b.astype(BF16), (((1,), (1,)), ((), ())),
                           preferred_element_type=F32)


def _dot_tn(a, b):
    return lax.dot_general(a.astype(BF16), b.astype(BF16), (((0,), (0,)), ((), ())),
                           preferred_element_type=F32)


def _split(a):
    hi = a.astype(BF16)
    lo = (a - hi.astype(F32)).astype(BF16)
    return hi, lo


def _dot_lhs2(a, b_bf16):
    hi, lo = _split(a)
    return (jnp.dot(hi, b_bf16, preferred_element_type=F32)
            + jnp.dot(lo, b_bf16, preferred_element_type=F32))


def _dot3(a, b):
    ah, al = _split(a)
    bh, bl = _split(b)
    return (jnp.dot(ah, bh, preferred_element_type=F32)
            + jnp.dot(al, bh, preferred_element_type=F32)
            + jnp.dot(ah, bl, preferred_element_type=F32))


def _rms(x, g):
    return x * lax.rsqrt(jnp.mean(x * x, axis=-1, keepdims=True) + NORM_EPS) * g


def _group_mat(n, group, value):
    idx = np.arange(n) // group
    return jnp.asarray((idx[:, None] == idx[None, :]).astype(np.float32) * value, dtype=BF16)


def _ffn_body(*refs, n_chunks, with_mix, with_final):
    it = iter(refs)
    x_ref = next(it)
    if with_mix:
        yf_ref, yb_ref, g_ref, nb_ref, yc_ref, wo_ref, cn_ref, pavg_ref = (next(it) for _ in range(8))
    ng_ref, win_ref, wout_ref = next(it), next(it), next(it)
    if with_final:
        fn_ref = next(it)
    o_ref = next(it)

    x = x_ref[...]
    if with_mix:
        ya = (yf_ref[...] + yb_ref[...]) * g_ref[...]
        yc = yc_ref[...]
        ms = _dot_lhs2(yc * yc, pavg_ref[...])
        yc = yc * lax.rsqrt(ms + NORM_EPS) * cn_ref[...]
        x = (x + _dot(ya, wo_ref[0:A_DIM, :]) + _dot(nb_ref[...], wo_ref[A_DIM:A_DIM + B_DIM, :])
             + _dot(yc, wo_ref[A_DIM + B_DIM:, :]))
    h = _rms(x, ng_ref[...]).astype(BF16)
    cw = D_FF // n_chunks
    acc = jnp.zeros_like(x)
    for c in range(n_chunks):
        gate = jnp.dot(h, win_ref[:, c * cw:(c + 1) * cw], preferred_element_type=F32)
        up = jnp.dot(h, win_ref[:, D_FF + c * cw:D_FF + (c + 1) * cw], preferred_element_type=F32)
        act = (gate * jax.nn.sigmoid(gate) * up).astype(BF16)
        acc = acc + jnp.dot(act, wout_ref[c * cw:(c + 1) * cw, :], preferred_element_type=F32)
    y = x + 0.5 * acc
    if with_final:
        y = _rms(y, fn_ref[...])
    o_ref[...] = y


def _ffn(x, norm_g, w_in, w_out, *, mix=None, final_g=None, tm=512, n_chunks=1):
    n = x.shape[0]
    tok = lambda w: pl.BlockSpec((tm, w), lambda i: (i, 0))
    args, specs = [x], [tok(D_MODEL)]
    if mix is not None:
        yf, yb, g, nb, yc, wo, cn = mix
        args += [yf, yb, g, nb, yc, wo, cn.reshape(1, C_DIM), _group_mat(C_DIM, HEAD_DIM, 1.0 / HEAD_DIM)]
        specs += [tok(A_DIM), tok(A_DIM), tok(A_DIM), tok(B_DIM), tok(C_DIM),
                  _const_spec((D_MODEL, D_MODEL)), _const_spec((1, C_DIM)), _const_spec((C_DIM, C_DIM))]
    args += [norm_g.reshape(1, D_MODEL), w_in, w_out]
    specs += [_const_spec((1, D_MODEL)), _const_spec((D_MODEL, 2 * D_FF)), _const_spec((D_FF, D_MODEL))]
    if final_g is not None:
        args.append(final_g.reshape(1, D_MODEL))
        specs.append(_const_spec((1, D_MODEL)))
    body = functools.partial(_ffn_body, n_chunks=n_chunks, with_mix=mix is not None,
                             with_final=final_g is not None)
    return pl.pallas_call(
        body, grid=(n // tm,), in_specs=specs, out_specs=tok(D_MODEL),
        out_shape=jax.ShapeDtypeStruct((n, D_MODEL), F32),
        compiler_params=_cparams(("parallel",)), name="ffn")(*args)


def _halo_specs(tm, width, n_tok):
    per = tm // SUBLANES
    last = n_tok // SUBLANES - 1
    prev = pl.BlockSpec((SUBLANES, width), lambda i: (jnp.maximum(i * per - 1, 0), 0))
    nxt = pl.BlockSpec((SUBLANES, width), lambda i: (jnp.minimum((i + 1) * per, last), 0))
    return prev, nxt


def _shifted(p, row_before, row_after, blocks_per_seq):
    tm = p.shape[0]
    j = pl.program_id(0) % blocks_per_seq
    hp = jnp.where(j == 0, 0.0, row_before)
    hn = jnp.where(j == blocks_per_seq - 1, 0.0, row_after)
    row = lax.broadcasted_iota(jnp.int32, p.shape, 0)
    prev = jnp.where(row == 0, hp, pltpu.roll(p, 1, axis=0))
    nxt = jnp.where(row == tm - 1, hn, pltpu.roll(p, tm - 1, axis=0))
    return prev, nxt


def _mixin_body(x_ref, xp_ref, xn_ref, g_ref, w_ref, mu_ref, wlo_ref, w0a0_ref, gup_ref, kk_ref, ka_ref, rk_ref,
                hsum_ref, cw_ref, cb_ref,
                pb_o, r_o, v_o, kk_o, g_o, lw0_o, lw1_o, kd0_o, kd1_o, b0_o, b1_o, bv0_o, bv1_o, hv_o, hx1_o, hx2_o,
                *, blocks_per_seq):
    a_cols = slice(0, A_PAD)
    c_cols = slice(A_PAD + B_COLS, A_PAD + B_COLS + C_COLS)
    h = _rms(x_ref[...], g_ref[...]).astype(BF16)
    pa = jnp.dot(h, w_ref[:, a_cols], preferred_element_type=F32)
    pb_o[...] = jnp.dot(h, w_ref[:, A_PAD:A_PAD + B_COLS], preferred_element_type=F32).astype(BF16)
    pc = jnp.dot(h, w_ref[:, c_cols], preferred_element_type=F32)
    hh = _rms(jnp.concatenate([xp_ref[...], xn_ref[...]], axis=0), g_ref[...]).astype(BF16)
    ha = jnp.dot(hh, w_ref[:, a_cols], preferred_element_type=F32)
    hc = jnp.dot(hh, w_ref[:, c_cols], preferred_element_type=F32)
    before, after = slice(SUBLANES - 1, SUBLANES), slice(SUBLANES, SUBLANES + 1)

    prev, nxt = _shifted(pc, hc[before], hc[after], blocks_per_seq)
    y = prev * cw_ref[0:1, :] + pc * cw_ref[1:2, :] + nxt * cw_ref[2:3, :] + cb_ref[...]
    hv_o[...] = y[:, 0:C_DIM]
    hx1_o[...] = y[:, C_DIM:2 * C_DIM]
    hx2_o[...] = y[:, 2 * C_DIM:]

    prev, nxt = _shifted(pa, ha[before], ha[after], blocks_per_seq)
    p = pa + mu_ref[...] * (0.5 * (prev + nxt) - pa)
    r = p[:, 0:A_DIM]
    k = p[:, A_DIM:2 * A_DIM]
    v = p[:, 2 * A_DIM:3 * A_DIM]
    lo = p[:, 3 * A_DIM:3 * A_DIM + 128]
    gd = p[:, 3 * A_DIM + 128:3 * A_DIM + 256]
    lane = lax.broadcasted_iota(jnp.int32, lo.shape, 1)
    lo = jnp.where(lane < 2 * A_LORA_W, jnp.tanh(lo), lo)
    wa = _dot3(lo, wlo_ref[...]) + w0a0_ref[...]
    g = _dot3(jax.nn.sigmoid(gd), gup_ref[...])
    hsum = hsum_ref[...]
    kkv = k * kk_ref[...]
    nrm = jnp.sqrt(_dot_lhs2(kkv * kkv, hsum))
    kkv = kkv / jnp.maximum(nrm, 1e-12)
    r_o[...] = r
    v_o[...] = v
    kk_o[...] = kkv
    g_o[...] = g
    for d, (lw_o, kd_o, b_o, bv_o) in enumerate(((lw0_o, kd0_o, b0_o, bv0_o), (lw1_o, kd1_o, b1_o, bv1_o))):
        w_log = -jax.nn.softplus(-wa[:, d * A_DIM:(d + 1) * A_DIM]) - 0.5
        a = jax.nn.sigmoid(wa[:, (2 + d) * A_DIM:(3 + d) * A_DIM])
        kd = k * (1.0 + (a - 1.0) * ka_ref[...])
        lw_o[...] = -jnp.exp(w_log)
        kd_o[...] = kd
        b_o[...] = kkv * a
        bv_o[...] = _dot_lhs2(r * kd * rk_ref[...], hsum) * v


def _mixin(x, lp, *, seq_len, tm=512):
    n = x.shape[0]
    tok = lambda w: pl.BlockSpec((tm, w), lambda i: (i, 0))
    prev, nxt = _halo_specs(tm, D_MODEL, n)
    w = lp['w_in']
    w_pad = jnp.concatenate([w[:, :A_COLS], jnp.zeros((D_MODEL, A_PAD - A_COLS), F32), w[:, A_COLS:]],
                            axis=1).astype(BF16)
    mu = jnp.pad(lp['rwkv_mu'], (0, A_PAD - A_COLS)).reshape(1, A_PAD)
    wlo = jnp.zeros((128, 4 * A_DIM), F32)
    for d in range(2):
        wlo = wlo.at[d * A_LORA_W:(d + 1) * A_LORA_W, d * A_DIM:(d + 1) * A_DIM].set(lp['rwkv_w_up'][d])
        wlo = wlo.at[64 + d * A_LORA_A:64 + (d + 1) * A_LORA_A, (2 + d) * A_DIM:(3 + d) * A_DIM].set(lp['rwkv_a_up'][d])
    w0a0 = jnp.concatenate([lp['rwkv_w0'].reshape(-1), lp['rwkv_a0'].reshape(-1)]).reshape(1, 4 * A_DIM)
    gup = jnp.pad(lp['rwkv_g_up'], ((0, 128 - A_LORA_G), (0, 0)))
    consts = [lp['norm_mix'].reshape(1, D_MODEL), w_pad, mu, wlo, w0a0, gup, lp['rwkv_k_k'].reshape(1, A_DIM),
              lp['rwkv_k_a'].reshape(1, A_DIM), lp['rwkv_r_k'].reshape(1, A_DIM), _group_mat(A_DIM, HEAD_DIM, 1.0),
              jnp.pad(lp['hy_conv_w'], ((0, SUBLANES - 3), (0, 0))), lp['hy_conv_b'].reshape(1, C_COLS)]
    body = functools.partial(_mixin_body, blocks_per_seq=seq_len // tm)
    outs = pl.pallas_call(
        body, grid=(n // tm,),
        in_specs=[tok(D_MODEL), prev, nxt] + [_const_spec(c.shape) for c in consts],
        out_specs=[tok(B_COLS)] + [tok(A_DIM)] * 12 + [tok(C_DIM)] * 3,
        out_shape=[jax.ShapeDtypeStruct((n, B_COLS), BF16)] + [jax.ShapeDtypeStruct((n, A_DIM), F32)] * 15,
        compiler_params=_cparams(("parallel",)), name="mixin")(x, x, x, *consts)
    return outs[0], outs[1:13], outs[13:16]


def _wkv_setup(r, v, kk, lw, kd, b, reverse):
    tb = r.shape[0]
    c_len = WKV_CHUNK
    pos = lax.broadcasted_iota(jnp.int32, (tb, A_DIM), 0) % c_len
    cum = lw
    sh = 1
    while sh < c_len:
        if reverse:
            cum = cum + jnp.where(pos < c_len - sh, pltpu.roll(cum, tb - sh, axis=0), 0.0)
        else:
            cum = cum + jnp.where(pos >= sh, pltpu.roll(cum, sh, axis=0), 0.0)
        sh *= 2
    e_neg = jnp.exp(-cum)
    return dict(cum=cum, rq=r * jnp.exp(cum), kq=kk * jnp.exp(cum - lw), kbar=kd * e_neg, bbar=b * e_neg, v=v)


def _wkv_body(rf, vf, kkf, lwf, kdf, bf, bvf, rb, vb, kkb, lwb, kdb, bb, bvb, lng_ref, lnb_ref, pavg_ref,
              of_ref, ob_ref, sf_ref, sb_ref):
    @pl.when(pl.program_id(1) == 0)
    def _():
        sf_ref[...] = jnp.zeros_like(sf_ref)
        sb_ref[...] = jnp.zeros_like(sb_ref)

    c_len = WKV_CHUNK
    tb = rf.shape[0]
    nc = tb // c_len
    dirs = (_wkv_setup(rf[...], vf[...], kkf[...], lwf[...], kdf[...], bf[...], False),
            _wkv_setup(rb[...], vb[...], kkb[...], lwb[...], kdb[...], bb[...], True))
    s_refs, o_refs, bvs = (sf_ref, sb_ref), (of_ref, ob_ref), (bvf, bvb)

    t_idx = lax.broadcasted_iota(jnp.int32, (c_len, A_DIM), 0)
    s_idx = lax.broadcasted_iota(jnp.int32, (c_len, A_DIM), 1) % c_len
    incl = (s_idx <= t_idx, s_idx >= t_idx)
    strict = (s_idx < t_idx, s_idx > t_idx)
    eye = (s_idx == t_idx).astype(F32)
    bd_mask = (lax.broadcasted_iota(jnp.int32, (A_DIM, A_DIM), 0) // HEAD_DIM
               == lax.broadcasted_iota(jnp.int32, (A_DIM, A_DIM), 1) // HEAD_DIM)
    zero_bf = jnp.zeros((A_DIM, A_DIM), BF16)

    def stack_hl(x_hl):
        return tuple(jnp.where(bd_mask, jnp.concatenate([part] * A_HEADS, axis=0), zero_bf) for part in x_hl)

    def stack(x):
        return stack_hl(_split(x))

    def mm_hl(a_hl, b_hl, nt=False):
        bh, bl = b_hl
        ah, al = a_hl
        m = ah.shape[0]
        dg = _dot_nt if nt else _dot
        p = dg(jnp.concatenate([ah, al], axis=0), bh)
        return p[0:m] + p[m:] + dg(ah, bl)

    def mm(a, b_hl, nt=False):
        return mm_hl(_split(a), b_hl, nt)

    items = [(d, c) for c in range(nc) for d in (0, 1)]
    ch = {}
    for d, c in items:
        sl = slice(c * c_len, (c + 1) * c_len)
        q = dirs[d]
        ch[d, c] = dict(rq=q['rq'][sl], kq=q['kq'][sl], kbar=q['kbar'][sl], bbar=q['bbar'][sl], v=q['v'][sl])
    for key in items:
        e = ch[key]
        kst, bst = stack(e['kbar']), stack(e['bbar'])
        acat = mm(jnp.concatenate([e['rq'], e['kq']], axis=0),
                  (jnp.concatenate([kst[0], bst[0]], axis=0), jnp.concatenate([kst[1], bst[1]], axis=0)),
                  nt=True)
        d = key[0]
        e['a_rv'] = jnp.concatenate([jnp.where(incl[d], acat[0:c_len, 0:A_DIM], 0.0),
                                     jnp.where(strict[d], acat[c_len:, 0:A_DIM], 0.0)], axis=0)
        e['a_rb'] = jnp.where(incl[d], acat[0:c_len, A_DIM:], 0.0)
        e['a_kb'] = jnp.where(strict[d], acat[c_len:, A_DIM:], 0.0)
        e['a_kb_hl'] = _split(e['a_kb'])
    half = 1
    zero_c = jnp.zeros((c_len, A_DIM), BF16)
    while half < c_len:
        in_block = (t_idx // (2 * half)) == (s_idx // (2 * half))
        lo_t, lo_s = t_idx % (2 * half) < half, s_idx % (2 * half) < half
        off = (in_block & ~lo_t & lo_s, in_block & lo_t & ~lo_s)
        if half == 1:
            for key in items:
                ch[key]['tinv'] = eye - jnp.where(off[key[0]], ch[key]['a_kb'], 0.0)
        else:
            for key in items:
                e = ch[key]
                e['tinv_hl'] = _split(e['tinv'])
                a_off_hl = tuple(jnp.where(off[key[0]], part, zero_c) for part in e['a_kb_hl'])
                e['tmp'] = mm_hl(e['tinv_hl'], stack_hl(a_off_hl))
            for key in items:
                e = ch[key]
                e['tinv'] = e['tinv'] - mm(e['tmp'], stack_hl(e['tinv_hl']))
        half *= 2
    for key in items:
        e = ch[key]
        av = mm(e['a_rv'], stack(e['v']))
        e['o_intra'], e['akkv'] = av[0:c_len], av[c_len:]
    for key in items:
        e = ch[key]
        t_hl = _split(e['tinv'])
        e['w_t'] = mm_hl(t_hl, stack(e['kq']))
        e['u0'] = mm_hl(t_hl, stack(e['akkv']))
    outs = ([None] * nc, [None] * nc)
    for step in range(nc):
        cur = [(0, step), (1, nc - 1 - step)]
        xs = {}
        for d, c in cur:
            e = ch[d, c]
            e['s_old'] = s_refs[d][...]
            xs[d] = mm(jnp.concatenate([e['rq'], e['w_t']], axis=0), _split(e['s_old']), nt=True)
        for d, c in cur:
            e = ch[d, c]
            u = -(xs[d][c_len:] + e['u0'])
            edge = c * c_len if d else (c + 1) * c_len - 1
            e_end = jnp.exp(dirs[d]['cum'][edge:edge + 1, :])
            vu_h, vu_l = _split(jnp.concatenate([e['v'], u], axis=0))
            kb_h, kb_l = _split(jnp.concatenate([e['kbar'] * e_end, e['bbar'] * e_end], axis=0))
            ds = _dot_tn(jnp.concatenate([vu_h, vu_l, vu_h], axis=0), jnp.concatenate([kb_h, kb_h, kb_l], axis=0))
            s_refs[d][...] = e['s_old'] * e_end + jnp.where(bd_mask, ds, 0.0)
            outs[d][c] = xs[d][0:c_len] + e['o_intra'] + mm(e['a_rb'], stack(u))
    lng, lnb, pavg = lng_ref[...], lnb_ref[...], pavg_ref[...]
    for d in (0, 1):
        o = jnp.concatenate(outs[d], axis=0)
        mu = _dot_lhs2(o, pavg)
        dev = o - mu
        var = _dot_lhs2(dev * dev, pavg)
        o_refs[d][...] = dev * lax.rsqrt(var + GN_EPS) * lng + lnb + bvs[d][...]


def _wkv(prep, lp, *, batch, seq_len, tb=512):
    r, v, kk, _, lw0, lw1, kd0, kd1, b0, b1, bv0, bv1 = prep
    n = r.shape[0]
    nblk = seq_len // tb
    fwd = pl.BlockSpec((tb, A_DIM), lambda bi, j: (bi * nblk + j, 0))
    bwd = pl.BlockSpec((tb, A_DIM), lambda bi, j: (bi * nblk + nblk - 1 - j, 0))
    consts = [lp['rwkv_ln_g'].reshape(1, A_DIM), lp['rwkv_ln_b'].reshape(1, A_DIM),
              _group_mat(A_DIM, HEAD_DIM, 1.0 / HEAD_DIM)]
    cspec = lambda shape: pl.BlockSpec(shape, lambda bi, j: (0, 0), pipeline_mode=pl.Buffered(1))
    return pl.pallas_call(
        _wkv_body, grid=(batch, nblk),
        in_specs=[fwd] * 7 + [bwd] * 7 + [cspec(c.shape) for c in consts],
        out_specs=[fwd, bwd],
        out_shape=[jax.ShapeDtypeStruct((n, A_DIM), F32)] * 2,
        scratch_shapes=[pltpu.VMEM((A_DIM, A_DIM), F32), pltpu.VMEM((A_DIM, A_DIM), F32)],
        compiler_params=_cparams(("parallel", "arbitrary")), name="wkv")(
            r, v, kk, lw0, kd0, b0, bv0, r, v, kk, lw1, kd1, b1, bv1, *consts)


def _na_tables(rpb, kh):
    col = np.arange(GRID_W)
    col_start = np.clip(col - NA_KW // 2, 0, GRID_W - NA_KW)
    col_ok = (col[None, :] >= col_start[:, None]) & (col[None, :] < col_start[:, None] + NA_KW)
    dx_idx = np.clip(col[None, :] - col[:, None] + NA_KW - 1, 0, 2 * NA_KW - 2)
    onehot = ((np.arange(2 * NA_KW - 1)[:, None, None] == dx_idx[None]) & col_ok[None]).astype(np.float32)
    mask_add = np.where(col_ok, 0.0, NEG_BIG).astype(np.float32)
    rpbx = jnp.einsum('hdx,xqk->hdqk', rpb.astype(F32), jnp.asarray(onehot),
                      precision=lax.Precision.HIGHEST) + jnp.asarray(mask_add)
    tabs = []
    for delta in range(kh):
        lo = NA_KH_MAX - 1 - delta
        tab = jnp.transpose(rpbx[:, lo:lo + kh], (0, 2, 1, 3))
        tabs.append(tab.reshape(B_HEADS * GRID_W, kh * GRID_W))
    return jnp.stack(tabs)


def _na_body(*refs, rows, kh, rq):
    q_ref, k_ref, v_ref = refs[0:3]
    bm_refs = refs[3:3 + rq]
    gn_ref, pavg_ref, o_ref = refs[3 + rq:]
    n_pairs = B_HEADS // 2
    pw = 2 * HEAD_DIM
    second = lax.broadcasted_iota(jnp.int32, (GRID_W, pw), 1) >= HEAD_DIM
    wins, ss = [], []
    for i in range(rq):
        r = pl.program_id(1) * rq + i
        rs = jnp.clip(r - kh // 2, 0, rows - kh)
        start = pl.multiple_of(rs * GRID_W, GRID_W)
        wins.append(start)
        parts = []
        for p in range(n_pairs):
            qp = q_ref[0, i * GRID_W:(i + 1) * GRID_W, p * pw:(p + 1) * pw]
            zero = jnp.zeros_like(qp)
            qs = jnp.concatenate([jnp.where(second, zero, qp), jnp.where(second, qp, zero)], axis=0)
            parts.append(_dot_nt(qs, k_ref[0, pl.ds(start, kh * GRID_W), p * pw:(p + 1) * pw]))
        ss.append(jnp.concatenate(parts, axis=0))
    ps, ls = [], []
    for i in range(rq):
        s = ss[i] * (HEAD_DIM ** -0.5) + bm_refs[i][0]
        m = jnp.max(s, axis=-1, keepdims=True)
        p = jnp.exp(s - m)
        ls.append(jnp.sum(p, axis=-1, keepdims=True))
        ps.append(p.astype(BF16))
    outs = []
    for i in range(rq):
        cols = []
        for p in range(n_pairs):
            rows_p = slice(p * 2 * GRID_W, (p + 1) * 2 * GRID_W)
            pv = _dot(ps[i][rows_p], v_ref[0, pl.ds(wins[i], kh * GRID_W), p * pw:(p + 1) * pw]) / ls[i][rows_p]
            cols.append(jnp.where(second, pv[GRID_W:], pv[0:GRID_W]))
        outs.append(jnp.concatenate(cols, axis=1))
    out = jnp.concatenate(outs, axis=0)
    ms = _dot_lhs2(out * out, pavg_ref[...])
    o_ref[0] = (out * lax.rsqrt(ms + NORM_EPS) * gn_ref[...]).astype(o_ref.dtype)


def _natten(pb, rpb, out_norm, *, batch, seq_len, rq=8):
    rows = seq_len // GRID_W
    kh = min(NA_KH_MAX, rows)
    tables = _na_tables(rpb, kh)
    pb3 = pb.reshape(batch, seq_len, B_COLS)

    def delta_map(i):
        def index(bi, j):
            r = j * rq + i
            return (r - jnp.clip(r - kh // 2, 0, rows - kh), 0, 0)
        return index

    body = functools.partial(_na_body, rows=rows, kh=kh, rq=rq)
    out = pl.pallas_call(
        body, grid=(batch, rows // rq),
        in_specs=[pl.BlockSpec((1, rq * GRID_W, B_DIM), lambda bi, j: (bi, j, 0)),
                  pl.BlockSpec((1, seq_len, B_DIM), lambda bi, j: (bi, 0, 1)),
                  pl.BlockSpec((1, seq_len, B_DIM), lambda bi, j: (bi, 0, 2))]
                 + [pl.BlockSpec((1, B_HEADS * GRID_W, kh * GRID_W), delta_map(i)) for i in range(rq)]
                 + [pl.BlockSpec((1, B_DIM), lambda bi, j: (0, 0)),
                    pl.BlockSpec((B_DIM, B_DIM), lambda bi, j: (0, 0))],
        out_specs=pl.BlockSpec((1, rq * GRID_W, B_DIM), lambda bi, j: (bi, j, 0)),
        out_shape=jax.ShapeDtypeStruct((batch, seq_len, B_DIM), BF16),
        compiler_params=_cparams(("parallel", "arbitrary")), name="natten")(
            pb3, pb3, pb3, *([tables] * rq), out_norm.reshape(1, B_DIM),
            _group_mat(B_DIM, HEAD_DIM, 1.0 / HEAD_DIM))
    return out.reshape(batch * seq_len, B_DIM)


def _hy_filter_body(z_ref, t_ref, w1_ref, b1_ref, w2_ref, b2_ref, w3_ref, fr_ref, dl_ref, k_o, s_o):
    fr = fr_ref[...]
    h = jnp.sin(fr * (_dot3(z_ref[...], w1_ref[...]) + b1_ref[...]))
    h = jnp.sin(fr * (_dot3(h, w2_ref[...]) + b2_ref[...]))
    h = _dot3(h, w3_ref[...])
    tcol = t_ref[...]
    kern = h * jnp.exp(-tcol[:, 0:1] * jnp.abs(dl_ref[...])) * tcol[:, 1:2]
    k_o[...] = kern

    @pl.when(pl.program_id(0) == 0)
    def _():
        s_o[...] = jnp.zeros_like(s_o)

    s_o[...] += jnp.sum(jnp.abs(kern), axis=0, keepdims=True)


def _hy_filter(lp, seq_len, *, rb=512):
    ln = seq_len
    pos = np.concatenate([np.arange(ln), np.array([0]), np.arange(ln - 1, 0, -1)]).astype(np.float64)
    valid = np.ones(2 * ln)
    valid[ln] = 0.0
    t32 = np.linspace(0.0, 1.0, ln, dtype=np.float32)[:, None]
    w32 = (np.float32(2.0 * math.pi) * np.arange(ln, dtype=np.float32)[:, None] / np.float32(ln)).astype(np.float32)
    f32 = np.linspace(1e-4, HY_BANDS - 1, HY_BANDS, dtype=np.float32)[None, :]
    z32 = np.concatenate([t32, np.cos(f32 * w32), -np.sin(f32 * w32)], axis=-1).astype(np.float32)
    idx = pos.astype(np.int64)
    z = np.zeros((2 * ln, 64), np.float32)
    z[:, :HY_EMB] = z32[idx]
    tcol = np.stack([t32[idx, 0], valid.astype(np.float32)], axis=-1)
    max_decay = math.log(HY_TARGET) / HY_FAST_DECAY_PCT
    min_decay = math.log(HY_TARGET) / HY_SLOW_DECAY_PCT
    deltas = np.linspace(min_decay, max_decay, C_DIM, dtype=np.float32)
    dl = jnp.asarray(np.tile(deltas, HY_ORDER)[None, :])
    w1 = jnp.pad(lp['hy_ff_w1'], ((0, 64 - HY_EMB), (0, 0)))
    half = ln // rb
    return pl.pallas_call(
        _hy_filter_body, grid=(2 * ln // rb,),
        in_specs=[pl.BlockSpec((rb, 64), lambda i: (i, 0)), pl.BlockSpec((rb, 2), lambda i: (i, 0)),
                  _const_spec((64, HY_FF)), _const_spec((1, HY_FF)), _const_spec((HY_FF, HY_FF)),
                  _const_spec((1, HY_FF)),
                  pl.BlockSpec((HY_FF, HY_ORDER * C_DIM), lambda i: (0, i // half)),
                  _const_spec((1, HY_FF)), _const_spec((1, HY_ORDER * C_DIM))],
        out_specs=[pl.BlockSpec((rb, HY_ORDER * C_DIM), lambda i: (i, 0)),
                   pl.BlockSpec((1, HY_ORDER * C_DIM), lambda i: (0, 0))],
        out_shape=[jax.ShapeDtypeStruct((2 * ln, HY_ORDER * C_DIM), F32),
                   jax.ShapeDtypeStruct((1, HY_ORDER * C_DIM), F32)],
        compiler_params=_cparams(("arbitrary",)), name="hy_filter")(
            jnp.asarray(z), jnp.asarray(tcol), w1, lp['hy_ff_b1'].reshape(1, HY_FF), lp['hy_ff_w2'],
            lp['hy_ff_b2'].reshape(1, HY_FF), lp['hy_ff_w3'], lp['hy_sin_freq'].reshape(1, HY_FF), dl)


def _dft_split(n):
    n2 = 64
    return n // n2, n2


def _dft_tables(n):
    n1, n2 = _dft_split(n)
    k1 = np.arange(n1)[:, None]
    m1 = np.arange(n1)[None, :]
    ang1 = 2.0 * np.pi * (k1 * m1 % n1) / n1
    f1 = np.stack([np.cos(ang1), -np.sin(ang1)], axis=1).reshape(2 * n1, n1)
    f1inv = np.stack([np.cos(ang1), -np.sin(ang1)], axis=2).reshape(n1, 2 * n1) / n
    eye = np.eye(DFT_ROWS)
    kk1 = np.arange(n1)[:, None, None]
    kk2 = np.arange(n2)[None, :, None]
    nn2 = np.arange(n2)[None, None, :]
    ang = 2.0 * np.pi * ((nn2 * (kk1 + n1 * kk2)) % n) / n
    c, s = np.cos(ang), np.sin(ang)
    g = np.concatenate([np.concatenate([c, s], axis=2), np.concatenate([-s, c], axis=2)], axis=1)
    as_bf = lambda a: jnp.asarray(a.astype(np.float32), dtype=BF16)
    return (as_bf(np.kron(f1, eye)), as_bf(np.kron(f1[:, 0:n1 // 2], eye)),
            as_bf(np.kron(f1inv[0:n1 // 2, :], eye)), as_bf(g))


def _hy_outer_body(f_ref, x_ref, o_ref):
    k1, rows, w = x_ref.shape[1:]
    x = x_ref[0].reshape(k1 * rows, w).astype(BF16)
    y = jnp.dot(f_ref[...], x, preferred_element_type=F32)
    o_ref[0] = y.reshape(y.shape[0] // rows, rows, w)


def _hy_outer(fk, x):
    bsz, k1, n2, w = x.shape
    m1 = fk.shape[0] // DFT_ROWS
    return pl.pallas_call(
        _hy_outer_body, grid=(bsz, n2 // DFT_ROWS),
        in_specs=[pl.BlockSpec(fk.shape, lambda bi, j: (0, 0), pipeline_mode=pl.Buffered(1)),
                  pl.BlockSpec((1, k1, DFT_ROWS, w), lambda bi, j: (bi, 0, j, 0))],
        out_specs=pl.BlockSpec((1, m1, DFT_ROWS, w), lambda bi, j: (bi, 0, j, 0)),
        out_shape=jax.ShapeDtypeStruct((bsz, m1, n2, w), F32),
        compiler_params=_cparams(("parallel", "parallel")), name="hy_outer")(fk, x)


def _hy_spec_body(g_ref, a_ref, inv_ref, h_ref):
    kb = g_ref.shape[0]
    for i in range(kb):
        h = jnp.dot(g_ref[i], a_ref[i].astype(BF16), preferred_element_type=F32) * inv_ref[...]
        h_ref[i] = h.astype(h_ref.dtype)


def _hy_spectrum(g, a, inv_norm, *, kb=8):
    n1, m, w = a.shape
    return pl.pallas_call(
        _hy_spec_body, grid=(n1 // kb,),
        in_specs=[pl.BlockSpec((kb, m, m), lambda i: (i, 0, 0)), pl.BlockSpec((kb, m, w), lambda i: (i, 0, 0)),
                  pl.BlockSpec((1, w), lambda i: (0, 0))],
        out_specs=pl.BlockSpec((kb, m, w), lambda i: (i, 0, 0)),
        out_shape=jax.ShapeDtypeStruct((n1, m, w), BF16),
        compiler_params=_cparams(("parallel",)), name="hy_spectrum")(g, a, inv_norm)


def _hy_conv_body(fk_ref, fki_ref, g_ref, h_ref, u_ref, x_ref, d_ref, o_ref, a_ref, *, kb):
    n1h, n2, c = u_ref.shape[1:]
    n1 = 2 * n1h
    blocks = [slice(j * DFT_ROWS, (j + 1) * DFT_ROWS) for j in range(n2 // DFT_ROWS)]
    fk = fk_ref[...]
    for rows in blocks:
        xb = u_ref[0, :, rows, :].reshape(n1h * DFT_ROWS, c).astype(BF16)
        a_ref[:, rows, :] = jnp.dot(fk, xb, preferred_element_type=F32).reshape(2 * n1, DFT_ROWS, c)

    def group(i, carry):
        ks = [i * kb + t for t in range(kb)]
        xs = [jnp.dot(g_ref[k], a_ref[pl.ds(2 * k, 2)].reshape(2 * n2, c).astype(BF16),
                      preferred_element_type=F32) for k in ks]
        zs = []
        for k, x in zip(ks, xs):
            xr, xi = x[0:n2], x[n2:]
            hr, hi = h_ref[k, 0:n2, :].astype(F32), h_ref[k, n2:, :].astype(F32)
            zs.append(jnp.concatenate([xr * hr - xi * hi, xr * hi + xi * hr], axis=0))
        for k, z in zip(ks, zs):
            a_ref[pl.ds(2 * k, 2)] = _dot_tn(g_ref[k], z).reshape(2, n2, c)
        return carry

    lax.fori_loop(0, n1 // kb, group, 0)
    fki = fki_ref[...]
    for rows in blocks:
        ab = a_ref[:, rows, :].reshape(2 * n1 * DFT_ROWS, c).astype(BF16)
        y = jnp.dot(fki, ab, preferred_element_type=F32).reshape(n1h, DFT_ROWS, c)
        o_ref[0, :, rows, :] = x_ref[0, :, rows, :] * (y + u_ref[0, :, rows, :] * d_ref[...])


def _hy_conv(fk_half, fk_inv, g, hspec, u, xg, dskip, order, *, kb=8):
    bsz, n1h, n2, c = u.shape
    n1 = 2 * n1h
    seq = lambda mode: pl.BlockSpec((1, n1h, n2, c), lambda bi: (bi, 0, 0, 0), pipeline_mode=mode)
    body = functools.partial(_hy_conv_body, kb=kb)
    return pl.pallas_call(
        body, grid=(bsz,),
        in_specs=[_const_spec(fk_half.shape), _const_spec(fk_inv.shape), _const_spec(g.shape),
                  pl.BlockSpec((n1, 2 * n2, c), lambda bi: (0, 0, order), pipeline_mode=pl.Buffered(1)),
                  seq(pl.Buffered(1)), seq(pl.Buffered(1)), _const_spec((1, c))],
        out_specs=pl.BlockSpec((1, n1h, n2, c), lambda bi: (bi, 0, 0, 0)),
        out_shape=jax.ShapeDtypeStruct(u.shape, F32),
        scratch_shapes=[pltpu.VMEM((2 * n1, n2, c), F32)],
        compiler_params=pltpu.CompilerParams(dimension_semantics=("parallel",), vmem_limit_bytes=HY_VMEM_LIMIT),
        name="hy_conv")(fk_half, fk_inv, g, hspec, u, xg, dskip)


def _hyena(vxx, lp, *, batch, seq_len):
    ln = seq_len
    n = 2 * ln
    n1, n2 = _dft_split(n)
    fk_full, fk_half, fk_inv, g = _dft_tables(n)
    v, x1, x2 = vxx
    kern, l1 = _hy_filter(lp, ln)
    ka = _hy_outer(fk_full, kern.reshape(1, n1, n2, HY_ORDER * C_DIM))
    hspec = _hy_spectrum(g, ka.reshape(n1, 2 * n2, HY_ORDER * C_DIM), 1.0 / l1)
    view = lambda t: t.reshape(batch, n1 // 2, n2, C_DIM)

    def conv_gate(u, xg, order):
        return _hy_conv(fk_half, fk_inv, g, hspec, u, xg, lp['hy_bias'][order:order + 1], order)

    z = conv_gate(view(v), view(x1), 0)
    y = conv_gate(z, view(x2), 1)
    return y.reshape(batch * ln, C_DIM)


def _trunk(x, params, final_norm):
    batch, seq_len, _ = x.shape
    x = x.reshape(batch * seq_len, D_MODEL)
    mix = None
    for layer in range(DEPTH):
        lp = {name: arr[layer] for name, arr in params.items()}
        if mix is not None:
            prev = {name: arr[layer - 1] for name, arr in params.items()}
            x = _ffn(x, prev['norm_ffn2'], prev['ffn2_w_in'].astype(BF16), prev['ffn2_w_out'].astype(BF16), mix=mix)
        x = _ffn(x, lp['norm_ffn1'], lp['ffn1_w_in'].astype(BF16), lp['ffn1_w_out'].astype(BF16))
        pb, prep, vxx = _mixin(x, lp, seq_len=seq_len)
        yf, yb = _wkv(prep, lp, batch=batch, seq_len=seq_len)
        nb = _natten(pb, lp['na_rpb'], lp['na_out_norm'], batch=batch, seq_len=seq_len)
        yc = _hyena(vxx, lp, batch=batch, seq_len=seq_len)
        mix = (yf, yb, prep[3], nb, yc, lp['w_out'].astype(BF16), lp['hy_out_norm'])
    last = {name: arr[DEPTH - 1] for name, arr in params.items()}
    x = _ffn(x, last['norm_ffn2'], last['ffn2_w_in'].astype(BF16), last['ffn2_w_out'].astype(BF16), mix=mix,
             final_g=final_norm)
    return x.reshape(batch, seq_len, D_MODEL)


def kernel(x_prompt, x_sample, norm_ffn1, ffn1_w_in, ffn1_w_out, norm_mix, w_in, rwkv_mu, rwkv_w0, rwkv_w_up, rwkv_a0, rwkv_a_up, rwkv_g_up, rwkv_k_k, rwkv_k_a, rwkv_r_k, rwkv_ln_g, rwkv_ln_b, na_rpb, na_out_norm, hy_conv_w, hy_conv_b, hy_ff_w1, hy_ff_b1, hy_ff_w2, hy_ff_b2, hy_ff_w3, hy_sin_freq, hy_bias, hy_out_norm, w_out, norm_ffn2, ffn2_w_in, ffn2_w_out, final_norm):
    params = {
        'norm_ffn1': norm_ffn1, 'ffn1_w_in': ffn1_w_in, 'ffn1_w_out': ffn1_w_out,
        'norm_mix': norm_mix, 'w_in': w_in,
        'rwkv_mu': rwkv_mu, 'rwkv_w0': rwkv_w0, 'rwkv_w_up': rwkv_w_up, 'rwkv_a0': rwkv_a0,
        'rwkv_a_up': rwkv_a_up, 'rwkv_g_up': rwkv_g_up, 'rwkv_k_k': rwkv_k_k, 'rwkv_k_a': rwkv_k_a,
        'rwkv_r_k': rwkv_r_k.reshape(DEPTH, A_DIM), 'rwkv_ln_g': rwkv_ln_g, 'rwkv_ln_b': rwkv_ln_b,
        'na_rpb': na_rpb, 'na_out_norm': na_out_norm,
        'hy_conv_w': hy_conv_w, 'hy_conv_b': hy_conv_b, 'hy_ff_w1': hy_ff_w1, 'hy_ff_b1': hy_ff_b1,
        'hy_ff_w2': hy_ff_w2, 'hy_ff_b2': hy_ff_b2, 'hy_ff_w3': hy_ff_w3, 'hy_sin_freq': hy_sin_freq,
        'hy_bias': hy_bias, 'hy_out_norm': hy_out_norm,
        'w_out': w_out, 'norm_ffn2': norm_ffn2, 'ffn2_w_in': ffn2_w_in, 'ffn2_w_out': ffn2_w_out,
    }
    return (_trunk(x_prompt, params, final_norm), _trunk(x_sample, params, final_norm))
```

```python
import functools
import math

import numpy as np
import jax
import jax.numpy as jnp
from jax import lax
from jax.experimental import pallas as pl
from jax.experimental.pallas import tpu as pltpu

F32 = jnp.float32
BF16 = jnp.bfloat16

D_MODEL = 1024
DEPTH = 4
GRID_W = 64
HEAD_DIM = 64
A_HEADS = 4
A_DIM = A_HEADS * HEAD_DIM
A_LORA_W = 32
A_LORA_A = 32
A_LORA_G = 64
B_HEADS = 8
B_DIM = B_HEADS * HEAD_DIM
NA_KH_MAX = 8
NA_KW = 16
C_DIM = D_MODEL - A_DIM - B_DIM
C_GROUPS = C_DIM // HEAD_DIM
HY_ORDER = 2
HY_BANDS = 16
HY_EMB = 1 + 2 * HY_BANDS
HY_FF = 64
HY_FAST_DECAY_PCT = 0.3
HY_SLOW_DECAY_PCT = 1.5
HY_TARGET = 1e-2
D_FF = 2816
NORM_EPS = 1e-5
GN_EPS = 64e-5
A_COLS = 3 * A_DIM + 2 * A_LORA_W + 2 * A_LORA_A + A_LORA_G
A_PAD = 1024
B_COLS = 3 * B_DIM
C_COLS = (HY_ORDER + 1) * C_DIM

SUBLANES = 8
VMEM_LIMIT = 56 * 1024 * 1024
HY_VMEM_LIMIT = 60 * 1024 * 1024
WKV_CHUNK = 64
NEG_BIG = -1e30
DFT_ROWS = SUBLANES


def _cparams(sem):
    return pltpu.CompilerParams(dimension_semantics=sem, vmem_limit_bytes=VMEM_LIMIT)


def _const_spec(shape):
    nd = len(shape)
    return pl.BlockSpec(shape, lambda *_: (0,) * nd, pipeline_mode=pl.Buffered(1))


def _dot(a, b):
    return jnp.dot(a.astype(BF16), b.astype(BF16), preferred_element_type=F32)


def _dot_nt(a, b):
    return lax.dot_general(a.astype(BF16), b.astype(BF16), (((1,), (1,)), ((), ())),
                           preferred_element_type=F32)


def _dot_tn(a, b):
    return lax.dot_general(a.astype(BF16), b.astype(BF16), (((0,), (0,)), ((), ())),
                           preferred_element_type=F32)


def _split(a):
    hi = a.astype(BF16)
    lo = (a - hi.astype(F32)).astype(BF16)
    return hi, lo


def _dot_lhs2(a, b_bf16):
    hi, lo = _split(a)
    return (jnp.dot(hi, b_bf16, preferred_element_type=F32)
            + jnp.dot(lo, b_bf16, preferred_element_type=F32))


def _dot3(a, b):
    ah, al = _split(a)
    bh, bl = _split(b)
    return (jnp.dot(ah, bh, preferred_element_type=F32)
            + jnp.dot(al, bh, preferred_element_type=F32)
            + jnp.dot(ah, bl, preferred_element_type=F32))


def _rms(x, g):
    return x * lax.rsqrt(jnp.mean(x * x, axis=-1, keepdims=True) + NORM_EPS) * g


def _group_mat(n, group, value):
    idx = np.arange(n) // group
    return jnp.asarray((idx[:, None] == idx[None, :]).astype(np.float32) * value, dtype=BF16)


def _ffn_body(*refs, n_chunks, with_mix, with_final):
    it = iter(refs)
    x_ref = next(it)
    if with_mix:
        yf_ref, yb_ref, g_ref, nb_ref, yc_ref, wo_ref, cn_ref, pavg_ref = (next(it) for _ in range(8))
    ng_ref, win_ref, wout_ref = next(it), next(it), next(it)
    if with_final:
        fn_ref = next(it)
    o_ref = next(it)

    x = x_ref[...]
    if with_mix:
        ya = (yf_ref[...] + yb_ref[...]) * g_ref[...]
        yc = yc_ref[...]
        ms = _dot_lhs2(yc * yc, pavg_ref[...])
        yc = yc * lax.rsqrt(ms + NORM_EPS) * cn_ref[...]
        x = (x + _dot(ya, wo_ref[0:A_DIM, :]) + _dot(nb_ref[...], wo_ref[A_DIM:A_DIM + B_DIM, :])
             + _dot(yc, wo_ref[A_DIM + B_DIM:, :]))
    h = _rms(x, ng_ref[...]).astype(BF16)
    cw = D_FF // n_chunks
    acc = jnp.zeros_like(x)
    for c in range(n_chunks):
        gate = jnp.dot(h, win_ref[:, c * cw:(c + 1) * cw], preferred_element_type=F32)
        up = jnp.dot(h, win_ref[:, D_FF + c * cw:D_FF + (c + 1) * cw], preferred_element_type=F32)
        act = (gate * jax.nn.sigmoid(gate) * up).astype(BF16)
        acc = acc + jnp.dot(act, wout_ref[c * cw:(c + 1) * cw, :], preferred_element_type=F32)
    y = x + 0.5 * acc
    if with_final:
        y = _rms(y, fn_ref[...])
    o_ref[...] = y


def _ffn(x, norm_g, w_in, w_out, *, mix=None, final_g=None, tm=512, n_chunks=1):
    n = x.shape[0]
    tok = lambda w: pl.BlockSpec((tm, w), lambda i: (i, 0))
    args, specs = [x], [tok(D_MODEL)]
    if mix is not None:
        yf, yb, g, nb, yc, wo, cn = mix
        args += [yf, yb, g, nb, yc, wo, cn.reshape(1, C_DIM), _group_mat(C_DIM, HEAD_DIM, 1.0 / HEAD_DIM)]
        specs += [tok(A_DIM), tok(A_DIM), tok(A_DIM), tok(B_DIM), tok(C_DIM),
                  _const_spec((D_MODEL, D_MODEL)), _const_spec((1, C_DIM)), _const_spec((C_DIM, C_DIM))]
    args += [norm_g.reshape(1, D_MODEL), w_in, w_out]
    specs += [_const_spec((1, D_MODEL)), _const_spec((D_MODEL, 2 * D_FF)), _const_spec((D_FF, D_MODEL))]
    if final_g is not None:
        args.append(final_g.reshape(1, D_MODEL))
        specs.append(_const_spec((1, D_MODEL)))
    body = functools.partial(_ffn_body, n_chunks=n_chunks, with_mix=mix is not None,
                             with_final=final_g is not None)
    return pl.pallas_call(
        body, grid=(n // tm,), in_specs=specs, out_specs=tok(D_MODEL),
        out_shape=jax.ShapeDtypeStruct((n, D_MODEL), F32),
        compiler_params=_cparams(("parallel",)), name="ffn")(*args)


def _halo_specs(tm, width, n_tok):
    per = tm // SUBLANES
    last = n_tok // SUBLANES - 1
    prev = pl.BlockSpec((SUBLANES, width), lambda i: (jnp.maximum(i * per - 1, 0), 0))
    nxt = pl.BlockSpec((SUBLANES, width), lambda i: (jnp.minimum((i + 1) * per, last), 0))
    return prev, nxt


def _shifted(p, row_before, row_after, blocks_per_seq):
    tm = p.shape[0]
    j = pl.program_id(0) % blocks_per_seq
    hp = jnp.where(j == 0, 0.0, row_before)
    hn = jnp.where(j == blocks_per_seq - 1, 0.0, row_after)
    row = lax.broadcasted_iota(jnp.int32, p.shape, 0)
    prev = jnp.where(row == 0, hp, pltpu.roll(p, 1, axis=0))
    nxt = jnp.where(row == tm - 1, hn, pltpu.roll(p, tm - 1, axis=0))
    return prev, nxt


def _mixin_body(x_ref, xp_ref, xn_ref, g_ref, w_ref, mu_ref, wlo_ref, w0a0_ref, gup_ref, kk_ref, ka_ref, rk_ref,
                hsum_ref, cw_ref, cb_ref,
                pb_o, r_o, v_o, kk_o, g_o, lw0_o, lw1_o, kd0_o, kd1_o, b0_o, b1_o, bv0_o, bv1_o, hv_o, hx1_o, hx2_o,
                *, blocks_per_seq):
    a_cols = slice(0, A_PAD)
    c_cols = slice(A_PAD + B_COLS, A_PAD + B_COLS + C_COLS)
    h = _rms(x_ref[...], g_ref[...]).astype(BF16)
    pa = jnp.dot(h, w_ref[:, a_cols], preferred_element_type=F32)
    pb_o[...] = jnp.dot(h, w_ref[:, A_PAD:A_PAD + B_COLS], preferred_element_type=F32).astype(BF16)
    pc = jnp.dot(h, w_ref[:, c_cols], preferred_element_type=F32)
    hh = _rms(jnp.concatenate([xp_ref[...], xn_ref[...]], axis=0), g_ref[...]).astype(BF16)
    ha = jnp.dot(hh, w_ref[:, a_cols], preferred_element_type=F32)
    hc = jnp.dot(hh, w_ref[:, c_cols], preferred_element_type=F32)
    before, after = slice(SUBLANES - 1, SUBLANES), slice(SUBLANES, SUBLANES + 1)

    prev, nxt = _shifted(pc, hc[before], hc[after], blocks_per_seq)
    y = prev * cw_ref[0:1, :] + pc * cw_ref[1:2, :] + nxt * cw_ref[2:3, :] + cb_ref[...]
    hv_o[...] = y[:, 0:C_DIM]
    hx1_o[...] = y[:, C_DIM:2 * C_DIM]
    hx2_o[...] = y[:, 2 * C_DIM:]

    prev, nxt = _shifted(pa, ha[before], ha[after], blocks_per_seq)
    p = pa + mu_ref[...] * (0.5 * (prev + nxt) - pa)
    r = p[:, 0:A_DIM]
    k = p[:, A_DIM:2 * A_DIM]
    v = p[:, 2 * A_DIM:3 * A_DIM]
    lo = p[:, 3 * A_DIM:3 * A_DIM + 128]
    gd = p[:, 3 * A_DIM + 128:3 * A_DIM + 256]
    lane = lax.broadcasted_iota(jnp.int32, lo.shape, 1)
    lo = jnp.where(lane < 2 * A_LORA_W, jnp.tanh(lo), lo)
    wa = _dot3(lo, wlo_ref[...]) + w0a0_ref[...]
    g = _dot3(jax.nn.sigmoid(gd), gup_ref[...])
    hsum = hsum_ref[...]
    kkv = k * kk_ref[...]
    nrm = jnp.sqrt(_dot_lhs2(kkv * kkv, hsum))
    kkv = kkv / jnp.maximum(nrm, 1e-12)
    r_o[...] = r
    v_o[...] = v
    kk_o[...] = kkv
    g_o[...] = g
    for d, (lw_o, kd_o, b_o, bv_o) in enumerate(((lw0_o, kd0_o, b0_o, bv0_o), (lw1_o, kd1_o, b1_o, bv1_o))):
        w_log = -jax.nn.softplus(-wa[:, d * A_DIM:(d + 1) * A_DIM]) - 0.5
        a = jax.nn.sigmoid(wa[:, (2 + d) * A_DIM:(3 + d) * A_DIM])
        kd = k * (1.0 + (a - 1.0) * ka_ref[...])
        lw_o[...] = -jnp.exp(w_log)
        kd_o[...] = kd
        b_o[...] = kkv * a
        bv_o[...] = _dot_lhs2(r * kd * rk_ref[...], hsum) * v


def _mixin(x, lp, *, seq_len, tm=512):
    n = x.shape[0]
    tok = lambda w: pl.BlockSpec((tm, w), lambda i: (i, 0))
    prev, nxt = _halo_specs(tm, D_MODEL, n)
    w_pad = lp['w_in_padded']
    mu = jnp.pad(lp['rwkv_mu'], (0, A_PAD - A_COLS)).reshape(1, A_PAD)
    wlo = jnp.zeros((128, 4 * A_DIM), F32)
    for d in range(2):
        wlo = wlo.at[d * A_LORA_W:(d + 1) * A_LORA_W, d * A_DIM:(d + 1) * A_DIM].set(lp['rwkv_w_up'][d])
        wlo = wlo.at[64 + d * A_LORA_A:64 + (d + 1) * A_LORA_A, (2 + d) * A_DIM:(3 + d) * A_DIM].set(lp['rwkv_a_up'][d])
    w0a0 = jnp.concatenate([lp['rwkv_w0'].reshape(-1), lp['rwkv_a0'].reshape(-1)]).reshape(1, 4 * A_DIM)
    gup = jnp.pad(lp['rwkv_g_up'], ((0, 128 - A_LORA_G), (0, 0)))
    consts = [lp['norm_mix'].reshape(1, D_MODEL), w_pad, mu, wlo, w0a0, gup, lp['rwkv_k_k'].reshape(1, A_DIM),
              lp['rwkv_k_a'].reshape(1, A_DIM), lp['rwkv_r_k'].reshape(1, A_DIM), _group_mat(A_DIM, HEAD_DIM, 1.0),
              jnp.pad(lp['hy_conv_w'], ((0, SUBLANES - 3), (0, 0))), lp['hy_conv_b'].reshape(1, C_COLS)]
    body = functools.partial(_mixin_body, blocks_per_seq=seq_len // tm)
    outs = pl.pallas_call(
        body, grid=(n // tm,),
        in_specs=[tok(D_MODEL), prev, nxt] + [_const_spec(c.shape) for c in consts],
        out_specs=[tok(B_COLS)] + [tok(A_DIM)] * 12 + [tok(C_DIM)] * 3,
        out_shape=[jax.ShapeDtypeStruct((n, B_COLS), BF16)] + [jax.ShapeDtypeStruct((n, A_DIM), F32)] * 15,
        compiler_params=_cparams(("parallel",)), name="mixin")(x, x, x, *consts)
    return outs[0], outs[1:13], outs[13:16]


def _wkv_setup(r, v, kk, lw, kd, b, reverse):
    tb = r.shape[0]
    c_len = WKV_CHUNK
    pos = lax.broadcasted_iota(jnp.int32, (tb, A_DIM), 0) % c_len
    cum = lw
    sh = 1
    while sh < c_len:
        if reverse:
            cum = cum + jnp.where(pos < c_len - sh, pltpu.roll(cum, tb - sh, axis=0), 0.0)
        else:
            cum = cum + jnp.where(pos >= sh, pltpu.roll(cum, sh, axis=0), 0.0)
        sh *= 2
    e_neg = jnp.exp(-cum)
    return dict(cum=cum, rq=r * jnp.exp(cum), kq=kk * jnp.exp(cum - lw), kbar=kd * e_neg, bbar=b * e_neg, v=v)


def _wkv_body(rf, vf, kkf, lwf, kdf, bf, bvf, rb, vb, kkb, lwb, kdb, bb, bvb, lng_ref, lnb_ref, pavg_ref,
              of_ref, ob_ref, sf_ref, sb_ref):
    @pl.when(pl.program_id(1) == 0)
    def _():
        sf_ref[...] = jnp.zeros_like(sf_ref)
        sb_ref[...] = jnp.zeros_like(sb_ref)

    c_len = WKV_CHUNK
    tb = rf.shape[0]
    nc = tb // c_len
    dirs = (_wkv_setup(rf[...], vf[...], kkf[...], lwf[...], kdf[...], bf[...], False),
            _wkv_setup(rb[...], vb[...], kkb[...], lwb[...], kdb[...], bb[...], True))
    s_refs, o_refs, bvs = (sf_ref, sb_ref), (of_ref, ob_ref), (bvf, bvb)

    t_idx = lax.broadcasted_iota(jnp.int32, (c_len, A_DIM), 0)
    s_idx = lax.broadcasted_iota(jnp.int32, (c_len, A_DIM), 1) % c_len
    incl = (s_idx <= t_idx, s_idx >= t_idx)
    strict = (s_idx < t_idx, s_idx > t_idx)
    eye = (s_idx == t_idx).astype(F32)
    bd_mask = (lax.broadcasted_iota(jnp.int32, (A_DIM, A_DIM), 0) // HEAD_DIM
               == lax.broadcasted_iota(jnp.int32, (A_DIM, A_DIM), 1) // HEAD_DIM)
    zero_bf = jnp.zeros((A_DIM, A_DIM), BF16)

    def stack_hl(x_hl):
        return tuple(jnp.where(bd_mask, jnp.concatenate([part] * A_HEADS, axis=0), zero_bf) for part in x_hl)

    def stack(x):
        return stack_hl(_split(x))

    def mm_hl(a_hl, b_hl, nt=False):
        bh, bl = b_hl
        ah, al = a_hl
        m = ah.shape[0]
        dg = _dot_nt if nt else _dot
        p = dg(jnp.concatenate([ah, al], axis=0), bh)
        return p[0:m] + p[m:] + dg(ah, bl)

    def mm(a, b_hl, nt=False):
        return mm_hl(_split(a), b_hl, nt)

    items = [(d, c) for c in range(nc) for d in (0, 1)]
    ch = {}
    for d, c in items:
        sl = slice(c * c_len, (c + 1) * c_len)
        q = dirs[d]
        ch[d, c] = dict(rq=q['rq'][sl], kq=q['kq'][sl], kbar=q['kbar'][sl], bbar=q['bbar'][sl], v=q['v'][sl])
    for key in items:
        e = ch[key]
        kst, bst = stack(e['kbar']), stack(e['bbar'])
        acat = mm(jnp.concatenate([e['rq'], e['kq']], axis=0),
                  (jnp.concatenate([kst[0], bst[0]], axis=0), jnp.concatenate([kst[1], bst[1]], axis=0)),
                  nt=True)
        d = key[0]
        e['a_rv'] = jnp.concatenate([jnp.where(incl[d], acat[0:c_len, 0:A_DIM], 0.0),
                                     jnp.where(strict[d], acat[c_len:, 0:A_DIM], 0.0)], axis=0)
        e['a_rb'] = jnp.where(incl[d], acat[0:c_len, A_DIM:], 0.0)
        e['a_kb'] = jnp.where(strict[d], acat[c_len:, A_DIM:], 0.0)
        e['a_kb_hl'] = _split(e['a_kb'])
    half = 1
    zero_c = jnp.zeros((c_len, A_DIM), BF16)
    while half < c_len:
        in_block = (t_idx // (2 * half)) == (s_idx // (2 * half))
        lo_t, lo_s = t_idx % (2 * half) < half, s_idx % (2 * half) < half
        off = (in_block & ~lo_t & lo_s, in_block & lo_t & ~lo_s)
        if half == 1:
            for key in items:
                ch[key]['tinv'] = eye - jnp.where(off[key[0]], ch[key]['a_kb'], 0.0)
        else:
            for key in items:
                e = ch[key]
                e['tinv_hl'] = _split(e['tinv'])
                a_off_hl = tuple(jnp.where(off[key[0]], part, zero_c) for part in e['a_kb_hl'])
                e['tmp'] = mm_hl(e['tinv_hl'], stack_hl(a_off_hl))
            for key in items:
                e = ch[key]
                e['tinv'] = e['tinv'] - mm(e['tmp'], stack_hl(e['tinv_hl']))
        half *= 2
    for key in items:
        e = ch[key]
        av = mm(e['a_rv'], stack(e['v']))
        e['o_intra'], e['akkv'] = av[0:c_len], av[c_len:]
    for key in items:
        e = ch[key]
        t_hl = _split(e['tinv'])
        e['w_t'] = mm_hl(t_hl, stack(e['kq']))
        e['u0'] = mm_hl(t_hl, stack(e['akkv']))
    outs = ([None] * nc, [None] * nc)
    for step in range(nc):
        cur = [(0, step), (1, nc - 1 - step)]
        xs = {}
        for d, c in cur:
            e = ch[d, c]
            e['s_old'] = s_refs[d][...]
            xs[d] = mm(jnp.concatenate([e['rq'], e['w_t']], axis=0), _split(e['s_old']), nt=True)
        for d, c in cur:
            e = ch[d, c]
            u = -(xs[d][c_len:] + e['u0'])
            edge = c * c_len if d else (c + 1) * c_len - 1
            e_end = jnp.exp(dirs[d]['cum'][edge:edge + 1, :])
            vu_h, vu_l = _split(jnp.concatenate([e['v'], u], axis=0))
            kb_h, kb_l = _split(jnp.concatenate([e['kbar'] * e_end, e['bbar'] * e_end], axis=0))
            ds = _dot_tn(jnp.concatenate([vu_h, vu_l, vu_h], axis=0), jnp.concatenate([kb_h, kb_h, kb_l], axis=0))
            s_refs[d][...] = e['s_old'] * e_end + jnp.where(bd_mask, ds, 0.0)
            outs[d][c] = xs[d][0:c_len] + e['o_intra'] + mm(e['a_rb'], stack(u))
    lng, lnb, pavg = lng_ref[...], lnb_ref[...], pavg_ref[...]
    for d in (0, 1):
        o = jnp.concatenate(outs[d], axis=0)
        mu = _dot_lhs2(o, pavg)
        dev = o - mu
        var = _dot_lhs2(dev * dev, pavg)
        o_refs[d][...] = dev * lax.rsqrt(var + GN_EPS) * lng + lnb + bvs[d][...]


def _wkv(prep, lp, *, batch, seq_len, tb=512):
    r, v, kk, _, lw0, lw1, kd0, kd1, b0, b1, bv0, bv1 = prep
    n = r.shape[0]
    nblk = seq_len // tb
    fwd = pl.BlockSpec((tb, A_DIM), lambda bi, j: (bi * nblk + j, 0))
    bwd = pl.BlockSpec((tb, A_DIM), lambda bi, j: (bi * nblk + nblk - 1 - j, 0))
    consts = [lp['rwkv_ln_g'].reshape(1, A_DIM), lp['rwkv_ln_b'].reshape(1, A_DIM),
              _group_mat(A_DIM, HEAD_DIM, 1.0 / HEAD_DIM)]
    cspec = lambda shape: pl.BlockSpec(shape, lambda bi, j: (0, 0), pipeline_mode=pl.Buffered(1))
    return pl.pallas_call(
        _wkv_body, grid=(batch, nblk),
        in_specs=[fwd] * 7 + [bwd] * 7 + [cspec(c.shape) for c in consts],
        out_specs=[fwd, bwd],
        out_shape=[jax.ShapeDtypeStruct((n, A_DIM), F32)] * 2,
        scratch_shapes=[pltpu.VMEM((A_DIM, A_DIM), F32), pltpu.VMEM((A_DIM, A_DIM), F32)],
        compiler_params=_cparams(("parallel", "arbitrary")), name="wkv")(
            r, v, kk, lw0, kd0, b0, bv0, r, v, kk, lw1, kd1, b1, bv1, *consts)


def _na_tables(rpb, kh):
    col = np.arange(GRID_W)
    col_start = np.clip(col - NA_KW // 2, 0, GRID_W - NA_KW)
    col_ok = (col[None, :] >= col_start[:, None]) & (col[None, :] < col_start[:, None] + NA_KW)
    dx_idx = np.clip(col[None, :] - col[:, None] + NA_KW - 1, 0, 2 * NA_KW - 2)
    dx_hot = ((np.arange(2 * NA_KW - 1)[:, None, None] == dx_idx[None]) & col_ok[None]).astype(np.float32)
    dy_idx = NA_KH_MAX - 1 - np.arange(kh)[:, None] + np.arange(kh)[None, :]
    dy_hot = (dy_idx[:, :, None] == np.arange(2 * NA_KH_MAX - 1)[None, None, :]).astype(np.float32)
    mask_add = np.where(col_ok, 0.0, NEG_BIG).astype(np.float32)[None, None, :, None, :]
    bias = jnp.einsum('ekd,hdx,xqc->ehqkc', jnp.asarray(dy_hot), rpb.astype(F32), jnp.asarray(dx_hot),
                      precision=lax.Precision.HIGHEST)
    return (bias + jnp.asarray(mask_add)).reshape(kh, B_HEADS * GRID_W, kh * GRID_W)


def _na_body(*refs, rows, kh, rq):
    q_ref, k_ref, v_ref = refs[0:3]
    bm_refs = refs[3:3 + rq]
    gn_ref, pavg_ref, o_ref = refs[3 + rq:]
    n_pairs = B_HEADS // 2
    pw = 2 * HEAD_DIM
    second = lax.broadcasted_iota(jnp.int32, (GRID_W, pw), 1) >= HEAD_DIM
    wins, ss = [], []
    for i in range(rq):
        r = pl.program_id(1) * rq + i
        rs = jnp.clip(r - kh // 2, 0, rows - kh)
        start = pl.multiple_of(rs * GRID_W, GRID_W)
        wins.append(start)
        parts = []
        for p in range(n_pairs):
            qp = q_ref[0, i * GRID_W:(i + 1) * GRID_W, p * pw:(p + 1) * pw]
            zero = jnp.zeros_like(qp)
            qs = jnp.concatenate([jnp.where(second, zero, qp), jnp.where(second, qp, zero)], axis=0)
            parts.append(_dot_nt(qs, k_ref[0, pl.ds(start, kh * GRID_W), p * pw:(p + 1) * pw]))
        ss.append(jnp.concatenate(parts, axis=0))
    ps, ls = [], []
    for i in range(rq):
        s = ss[i] * (HEAD_DIM ** -0.5) + bm_refs[i][0]
        m = jnp.max(s, axis=-1, keepdims=True)
        p = jnp.exp(s - m)
        ls.append(jnp.sum(p, axis=-1, keepdims=True))
        ps.append(p.astype(BF16))
    outs = []
    for i in range(rq):
        cols = []
        for p in range(n_pairs):
            rows_p = slice(p * 2 * GRID_W, (p + 1) * 2 * GRID_W)
            pv = _dot(ps[i][rows_p], v_ref[0, pl.ds(wins[i], kh * GRID_W), p * pw:(p + 1) * pw]) / ls[i][rows_p]
            cols.append(jnp.where(second, pv[GRID_W:], pv[0:GRID_W]))
        outs.append(jnp.concatenate(cols, axis=1))
    out = jnp.concatenate(outs, axis=0)
    ms = _dot_lhs2(out * out, pavg_ref[...])
    o_ref[0] = (out * lax.rsqrt(ms + NORM_EPS) * gn_ref[...]).astype(o_ref.dtype)


def _natten(pb, rpb, out_norm, *, batch, seq_len, rq=8):
    rows = seq_len // GRID_W
    kh = min(NA_KH_MAX, rows)
    tables = _na_tables(rpb, kh)
    pb3 = pb.reshape(batch, seq_len, B_COLS)

    def delta_map(i):
        def index(bi, j):
            r = j * rq + i
            return (r - jnp.clip(r - kh // 2, 0, rows - kh), 0, 0)
        return index

    body = functools.partial(_na_body, rows=rows, kh=kh, rq=rq)
    out = pl.pallas_call(
        body, grid=(batch, rows // rq),
        in_specs=[pl.BlockSpec((1, rq * GRID_W, B_DIM), lambda bi, j: (bi, j, 0)),
                  pl.BlockSpec((1, seq_len, B_DIM), lambda bi, j: (bi, 0, 1)),
                  pl.BlockSpec((1, seq_len, B_DIM), lambda bi, j: (bi, 0, 2))]
                 + [pl.BlockSpec((1, B_HEADS * GRID_W, kh * GRID_W), delta_map(i)) for i in range(rq)]
                 + [pl.BlockSpec((1, B_DIM), lambda bi, j: (0, 0)),
                    pl.BlockSpec((B_DIM, B_DIM), lambda bi, j: (0, 0))],
        out_specs=pl.BlockSpec((1, rq * GRID_W, B_DIM), lambda bi, j: (bi, j, 0)),
        out_shape=jax.ShapeDtypeStruct((batch, seq_len, B_DIM), BF16),
        compiler_params=_cparams(("parallel", "arbitrary")), name="natten")(
            pb3, pb3, pb3, *([tables] * rq), out_norm.reshape(1, B_DIM),
            _group_mat(B_DIM, HEAD_DIM, 1.0 / HEAD_DIM))
    return out.reshape(batch * seq_len, B_DIM)


def _hy_filter_body(z_ref, t_ref, w1_ref, b1_ref, w2_ref, b2_ref, w3_ref, fr_ref, dl_ref, k_o, s_o):
    fr = fr_ref[...]
    h = jnp.sin(fr * (_dot3(z_ref[...], w1_ref[...]) + b1_ref[...]))
    h = jnp.sin(fr * (_dot3(h, w2_ref[...]) + b2_ref[...]))
    h = _dot3(h, w3_ref[...])
    tcol = t_ref[...]
    kern = h * jnp.exp(-tcol[:, 0:1] * jnp.abs(dl_ref[...])) * tcol[:, 1:2]
    k_o[...] = kern

    @pl.when(pl.program_id(0) == 0)
    def _():
        s_o[...] = jnp.zeros_like(s_o)

    s_o[...] += jnp.sum(jnp.abs(kern), axis=0, keepdims=True)


def _hy_filter(lp, seq_len, *, rb=512):
    ln = seq_len
    pos = np.concatenate([np.arange(ln), np.array([0]), np.arange(ln - 1, 0, -1)]).astype(np.float64)
    valid = np.ones(2 * ln)
    valid[ln] = 0.0
    t32 = np.linspace(0.0, 1.0, ln, dtype=np.float32)[:, None]
    w32 = (np.float32(2.0 * math.pi) * np.arange(ln, dtype=np.float32)[:, None] / np.float32(ln)).astype(np.float32)
    f32 = np.linspace(1e-4, HY_BANDS - 1, HY_BANDS, dtype=np.float32)[None, :]
    z32 = np.concatenate([t32, np.cos(f32 * w32), -np.sin(f32 * w32)], axis=-1).astype(np.float32)
    idx = pos.astype(np.int64)
    z = np.zeros((2 * ln, 64), np.float32)
    z[:, :HY_EMB] = z32[idx]
    tcol = np.stack([t32[idx, 0], valid.astype(np.float32)], axis=-1)
    max_decay = math.log(HY_TARGET) / HY_FAST_DECAY_PCT
    min_decay = math.log(HY_TARGET) / HY_SLOW_DECAY_PCT
    deltas = np.linspace(min_decay, max_decay, C_DIM, dtype=np.float32)
    dl = jnp.asarray(np.tile(deltas, HY_ORDER)[None, :])
    w1 = jnp.pad(lp['hy_ff_w1'], ((0, 64 - HY_EMB), (0, 0)))
    half = ln // rb
    return pl.pallas_call(
        _hy_filter_body, grid=(2 * ln // rb,),
        in_specs=[pl.BlockSpec((rb, 64), lambda i: (i, 0)), pl.BlockSpec((rb, 2), lambda i: (i, 0)),
                  _const_spec((64, HY_FF)), _const_spec((1, HY_FF)), _const_spec((HY_FF, HY_FF)),
                  _const_spec((1, HY_FF)),
                  pl.BlockSpec((HY_FF, HY_ORDER * C_DIM), lambda i: (0, i // half)),
                  _const_spec((1, HY_FF)), _const_spec((1, HY_ORDER * C_DIM))],
        out_specs=[pl.BlockSpec((rb, HY_ORDER * C_DIM), lambda i: (i, 0)),
                   pl.BlockSpec((1, HY_ORDER * C_DIM), lambda i: (0, 0))],
        out_shape=[jax.ShapeDtypeStruct((2 * ln, HY_ORDER * C_DIM), F32),
                   jax.ShapeDtypeStruct((1, HY_ORDER * C_DIM), F32)],
        compiler_params=_cparams(("arbitrary",)), name="hy_filter")(
            jnp.asarray(z), jnp.asarray(tcol), w1, lp['hy_ff_b1'].reshape(1, HY_FF), lp['hy_ff_w2'],
            lp['hy_ff_b2'].reshape(1, HY_FF), lp['hy_ff_w3'], lp['hy_sin_freq'].reshape(1, HY_FF), dl)


def _dft_split(n):
    n2 = 64
    return n // n2, n2


def _dft_tables(n):
    n1, n2 = _dft_split(n)
    k1 = np.arange(n1)[:, None]
    m1 = np.arange(n1)[None, :]
    ang1 = 2.0 * np.pi * (k1 * m1 % n1) / n1
    f1 = np.stack([np.cos(ang1), -np.sin(ang1)], axis=1).reshape(2 * n1, n1)
    f1inv = np.stack([np.cos(ang1), -np.sin(ang1)], axis=2).reshape(n1, 2 * n1) / n
    eye = np.eye(DFT_ROWS)
    kk1 = np.arange(n1)[:, None, None]
    kk2 = np.arange(n2)[None, :, None]
    nn2 = np.arange(n2)[None, None, :]
    ang = 2.0 * np.pi * ((nn2 * (kk1 + n1 * kk2)) % n) / n
    c, s = np.cos(ang), np.sin(ang)
    g = np.concatenate([np.concatenate([c, s], axis=2), np.concatenate([-s, c], axis=2)], axis=1)
    as_bf = lambda a: jnp.asarray(a.astype(np.float32), dtype=BF16)
    return (as_bf(np.kron(f1, eye)), as_bf(np.kron(f1[:, 0:n1 // 2], eye)),
            as_bf(np.kron(f1inv[0:n1 // 2, :], eye)), as_bf(g))


def _hy_outer_body(f_ref, x_ref, o_ref):
    k1, rows, w = x_ref.shape[1:]
    x = x_ref[0].reshape(k1 * rows, w).astype(BF16)
    y = jnp.dot(f_ref[...], x, preferred_element_type=F32)
    o_ref[0] = y.reshape(y.shape[0] // rows, rows, w)


def _hy_outer(fk, x):
    bsz, k1, n2, w = x.shape
    m1 = fk.shape[0] // DFT_ROWS
    return pl.pallas_call(
        _hy_outer_body, grid=(bsz, n2 // DFT_ROWS),
        in_specs=[pl.BlockSpec(fk.shape, lambda bi, j: (0, 0), pipeline_mode=pl.Buffered(1)),
                  pl.BlockSpec((1, k1, DFT_ROWS, w), lambda bi, j: (bi, 0, j, 0))],
        out_specs=pl.BlockSpec((1, m1, DFT_ROWS, w), lambda bi, j: (bi, 0, j, 0)),
        out_shape=jax.ShapeDtypeStruct((bsz, m1, n2, w), F32),
        compiler_params=_cparams(("parallel", "parallel")), name="hy_outer")(fk, x)


def _hy_spec_body(g_ref, a_ref, inv_ref, h_ref):
    kb = g_ref.shape[0]
    for i in range(kb):
        h = jnp.dot(g_ref[i], a_ref[i].astype(BF16), preferred_element_type=F32) * inv_ref[...]
        h_ref[i] = h.astype(h_ref.dtype)


def _hy_spectrum(g, a, inv_norm, *, kb=8):
    n1, m, w = a.shape
    return pl.pallas_call(
        _hy_spec_body, grid=(n1 // kb,),
        in_specs=[pl.BlockSpec((kb, m, m), lambda i: (i, 0, 0)), pl.BlockSpec((kb, m, w), lambda i: (i, 0, 0)),
                  pl.BlockSpec((1, w), lambda i: (0, 0))],
        out_specs=pl.BlockSpec((kb, m, w), lambda i: (i, 0, 0)),
        out_shape=jax.ShapeDtypeStruct((n1, m, w), BF16),
        compiler_params=_cparams(("parallel",)), name="hy_spectrum")(g, a, inv_norm)


def _hy_conv_body(fk_ref, fki_ref, g_ref, h_ref, u_ref, x_ref, d_ref, o_ref, a_ref, *, kb):
    n1h, n2, c = u_ref.shape[1:]
    n1 = 2 * n1h
    blocks = [slice(j * DFT_ROWS, (j + 1) * DFT_ROWS) for j in range(n2 // DFT_ROWS)]
    fk = fk_ref[...]
    for rows in blocks:
        xb = u_ref[0, :, rows, :].reshape(n1h * DFT_ROWS, c).astype(BF16)
        a_ref[:, rows, :] = jnp.dot(fk, xb, preferred_element_type=F32).reshape(2 * n1, DFT_ROWS, c)

    def group(i, carry):
        ks = [i * kb + t for t in range(kb)]
        xs = [jnp.dot(g_ref[k], a_ref[pl.ds(2 * k, 2)].reshape(2 * n2, c).astype(BF16),
                      preferred_element_type=F32) for k in ks]
        zs = []
        for k, x in zip(ks, xs):
            xr, xi = x[0:n2], x[n2:]
            hr, hi = h_ref[k, 0:n2, :].astype(F32), h_ref[k, n2:, :].astype(F32)
            zs.append(jnp.concatenate([xr * hr - xi * hi, xr * hi + xi * hr], axis=0))
        for k, z in zip(ks, zs):
            a_ref[pl.ds(2 * k, 2)] = _dot_tn(g_ref[k], z).reshape(2, n2, c)
        return carry

    lax.fori_loop(0, n1 // kb, group, 0)
    fki = fki_ref[...]
    for rows in blocks:
        ab = a_ref[:, rows, :].reshape(2 * n1 * DFT_ROWS, c).astype(BF16)
        y = jnp.dot(fki, ab, preferred_element_type=F32).reshape(n1h, DFT_ROWS, c)
        o_ref[0, :, rows, :] = x_ref[0, :, rows, :] * (y + u_ref[0, :, rows, :] * d_ref[...])


def _hy_conv(fk_half, fk_inv, g, hspec, u, xg, dskip, order, *, kb=8):
    bsz, n1h, n2, c = u.shape
    n1 = 2 * n1h
    seq = lambda mode: pl.BlockSpec((1, n1h, n2, c), lambda bi: (bi, 0, 0, 0), pipeline_mode=mode)
    body = functools.partial(_hy_conv_body, kb=kb)
    return pl.pallas_call(
        body, grid=(bsz,),
        in_specs=[_const_spec(fk_half.shape), _const_spec(fk_inv.shape), _const_spec(g.shape),
                  pl.BlockSpec((n1, 2 * n2, c), lambda bi: (0, 0, order), pipeline_mode=pl.Buffered(1)),
                  seq(pl.Buffered(1)), seq(pl.Buffered(1)), _const_spec((1, c))],
        out_specs=pl.BlockSpec((1, n1h, n2, c), lambda bi: (bi, 0, 0, 0)),
        out_shape=jax.ShapeDtypeStruct(u.shape, F32),
        scratch_shapes=[pltpu.VMEM((2 * n1, n2, c), F32)],
        compiler_params=pltpu.CompilerParams(dimension_semantics=("parallel",), vmem_limit_bytes=HY_VMEM_LIMIT),
        name="hy_conv")(fk_half, fk_inv, g, hspec, u, xg, dskip)


def _hyena(vxx, lp, *, batch, seq_len):
    ln = seq_len
    n = 2 * ln
    n1, n2 = _dft_split(n)
    fk_full, fk_half, fk_inv, g = _dft_tables(n)
    v, x1, x2 = vxx
    kern, l1 = _hy_filter(lp, ln)
    ka = _hy_outer(fk_full, kern.reshape(1, n1, n2, HY_ORDER * C_DIM))
    hspec = _hy_spectrum(g, ka.reshape(n1, 2 * n2, HY_ORDER * C_DIM), 1.0 / l1)
    view = lambda t: t.reshape(batch, n1 // 2, n2, C_DIM)

    def conv_gate(u, xg, order):
        return _hy_conv(fk_half, fk_inv, g, hspec, u, xg, lp['hy_bias'][order:order + 1], order)

    z = conv_gate(view(v), view(x1), 0)
    y = conv_gate(z, view(x2), 1)
    return y.reshape(batch * ln, C_DIM)


def _trunk(x, params, final_norm):
    batch, seq_len, _ = x.shape
    x = x.reshape(batch * seq_len, D_MODEL)
    w = params['w_in']
    params = dict(params, w_in_padded=jnp.concatenate(
        [w[:, :, :A_COLS], jnp.zeros((DEPTH, D_MODEL, A_PAD - A_COLS), F32), w[:, :, A_COLS:]], axis=2).astype(BF16))
    mix = None
    for layer in range(DEPTH):
        lp = {name: arr[layer] for name, arr in params.items()}
        if mix is not None:
            prev = {name: arr[layer - 1] for name, arr in params.items()}
            x = _ffn(x, prev['norm_ffn2'], prev['ffn2_w_in'].astype(BF16), prev['ffn2_w_out'].astype(BF16), mix=mix)
        x = _ffn(x, lp['norm_ffn1'], lp['ffn1_w_in'].astype(BF16), lp['ffn1_w_out'].astype(BF16))
        pb, prep, vxx = _mixin(x, lp, seq_len=seq_len)
        yf, yb = _wkv(prep, lp, batch=batch, seq_len=seq_len)
        nb = _natten(pb, lp['na_rpb'], lp['na_out_norm'], batch=batch, seq_len=seq_len)
        yc = _hyena(vxx, lp, batch=batch, seq_len=seq_len)
        mix = (yf, yb, prep[3], nb, yc, lp['w_out'].astype(BF16), lp['hy_out_norm'])
    last = {name: arr[DEPTH - 1] for name, arr in params.items()}
    x = _ffn(x, last['norm_ffn2'], last['ffn2_w_in'].astype(BF16), last['ffn2_w_out'].astype(BF16), mix=mix,
             final_g=final_norm)
    return x.reshape(batch, seq_len, D_MODEL)


def kernel(x_prompt, x_sample, norm_ffn1, ffn1_w_in, ffn1_w_out, norm_mix, w_in, rwkv_mu, rwkv_w0, rwkv_w_up, rwkv_a0, rwkv_a_up, rwkv_g_up, rwkv_k_k, rwkv_k_a, rwkv_r_k, rwkv_ln_g, rwkv_ln_b, na_rpb, na_out_norm, hy_conv_w, hy_conv_b, hy_ff_w1, hy_ff_b1, hy_ff_w2, hy_ff_b2, hy_ff_w3, hy_sin_freq, hy_bias, hy_out_norm, w_out, norm_ffn2, ffn2_w_in, ffn2_w_out, final_norm):
    params = {
        'norm_ffn1': norm_ffn1, 'ffn1_w_in': ffn1_w_in, 'ffn1_w_out': ffn1_w_out,
        'norm_mix': norm_mix, 'w_in': w_in,
        'rwkv_mu': rwkv_mu, 'rwkv_w0': rwkv_w0, 'rwkv_w_up': rwkv_w_up, 'rwkv_a0': rwkv_a0,
        'rwkv_a_up': rwkv_a_up, 'rwkv_g_up': rwkv_g_up, 'rwkv_k_k': rwkv_k_k, 'rwkv_k_a': rwkv_k_a,
        'rwkv_r_k': rwkv_r_k.reshape(DEPTH, A_DIM), 'rwkv_ln_g': rwkv_ln_g, 'rwkv_ln_b': rwkv_ln_b,
        'na_rpb': na_rpb, 'na_out_norm': na_out_norm,
        'hy_conv_w': hy_conv_w, 'hy_conv_b': hy_conv_b, 'hy_ff_w1': hy_ff_w1, 'hy_ff_b1': hy_ff_b1,
        'hy_ff_w2': hy_ff_w2, 'hy_ff_b2': hy_ff_b2, 'hy_ff_w3': hy_ff_w3, 'hy_sin_freq': hy_sin_freq,
        'hy_bias': hy_bias, 'hy_out_norm': hy_out_norm,
        'w_out': w_out, 'norm_ffn2': norm_ffn2, 'ffn2_w_in': ffn2_w_in, 'ffn2_w_out': ffn2_w_out,
    }
    return (_trunk(x_prompt, params, final_norm), _trunk(x_sample, params, final_norm))
```

```python
import functools
import math

import numpy as np
import jax
import jax.numpy as jnp
from jax import lax
from jax.experimental import pallas as pl
from jax.experimental.pallas import tpu as pltpu

F32 = jnp.float32
BF16 = jnp.bfloat16

D_MODEL = 1024
DEPTH = 4
GRID_W = 64
HEAD_DIM = 64
A_HEADS = 4
A_DIM = A_HEADS * HEAD_DIM
A_LORA_W = 32
A_LORA_A = 32
A_LORA_G = 64
B_HEADS = 8
B_DIM = B_HEADS * HEAD_DIM
NA_KH_MAX = 8
NA_KW = 16
C_DIM = D_MODEL - A_DIM - B_DIM
C_GROUPS = C_DIM // HEAD_DIM
HY_ORDER = 2
HY_BANDS = 16
HY_EMB = 1 + 2 * HY_BANDS
HY_FF = 64
HY_FAST_DECAY_PCT = 0.3
HY_SLOW_DECAY_PCT = 1.5
HY_TARGET = 1e-2
D_FF = 2816
NORM_EPS = 1e-5
GN_EPS = 64e-5
A_COLS = 3 * A_DIM + 2 * A_LORA_W + 2 * A_LORA_A + A_LORA_G
A_PAD = 1024
B_COLS = 3 * B_DIM
C_COLS = (HY_ORDER + 1) * C_DIM

SUBLANES = 8
VMEM_LIMIT = 56 * 1024 * 1024
HY_VMEM_LIMIT = 60 * 1024 * 1024
WKV_CHUNK = 64
NEG_BIG = -1e30
DFT_ROWS = SUBLANES


def _cparams(sem):
    return pltpu.CompilerParams(dimension_semantics=sem, vmem_limit_bytes=VMEM_LIMIT)


def _const_spec(shape):
    nd = len(shape)
    return pl.BlockSpec(shape, lambda *_: (0,) * nd, pipeline_mode=pl.Buffered(1))


def _dot(a, b):
    return jnp.dot(a.astype(BF16), b.astype(BF16), preferred_element_type=F32)


def _dot_nt(a, b):
    return lax.dot_general(a.astype(BF16), b.astype(BF16), (((1,), (1,)), ((), ())),
                           preferred_element_type=F32)


def _dot_tn(a, b):
    return lax.dot_general(a.astype(BF16), b.astype(BF16), (((0,), (0,)), ((), ())),
                           preferred_element_type=F32)


def _split(a):
    hi = a.astype(BF16)
    lo = (a - hi.astype(F32)).astype(BF16)
    return hi, lo


def _dot_lhs2(a, b_bf16):
    hi, lo = _split(a)
    return (jnp.dot(hi, b_bf16, preferred_element_type=F32)
            + jnp.dot(lo, b_bf16, preferred_element_type=F32))


def _dot3(a, b):
    ah, al = _split(a)
    bh, bl = _split(b)
    return (jnp.dot(ah, bh, preferred_element_type=F32)
            + jnp.dot(al, bh, preferred_element_type=F32)
            + jnp.dot(ah, bl, preferred_element_type=F32))


def _rms(x, g):
    return x * lax.rsqrt(jnp.mean(x * x, axis=-1, keepdims=True) + NORM_EPS) * g


def _group_mat(n, group, value):
    idx = np.arange(n) // group
    return jnp.asarray((idx[:, None] == idx[None, :]).astype(np.float32) * value, dtype=BF16)


def _ffn_body(*refs, n_chunks, with_mix, with_final):
    it = iter(refs)
    x_ref = next(it)
    if with_mix:
        yf_ref, yb_ref, g_ref, nb_ref, yc_ref, wo_ref, cn_ref, pavg_ref = (next(it) for _ in range(8))
    ng_ref, win_ref, wout_ref = next(it), next(it), next(it)
    if with_final:
        fn_ref = next(it)
    o_ref = next(it)

    x = x_ref[...]
    if with_mix:
        ya = (yf_ref[...] + yb_ref[...]) * g_ref[...]
        yc = yc_ref[...]
        ms = _dot_lhs2(yc * yc, pavg_ref[...])
        yc = yc * lax.rsqrt(ms + NORM_EPS) * cn_ref[...]
        x = (x + _dot(ya, wo_ref[0:A_DIM, :]) + _dot(nb_ref[...], wo_ref[A_DIM:A_DIM + B_DIM, :])
             + _dot(yc, wo_ref[A_DIM + B_DIM:, :]))
    h = _rms(x, ng_ref[...]).astype(BF16)
    cw = D_FF // n_chunks
    acc = jnp.zeros_like(x)
    for c in range(n_chunks):
        gate = jnp.dot(h, win_ref[:, c * cw:(c + 1) * cw], preferred_element_type=F32)
        up = jnp.dot(h, win_ref[:, D_FF + c * cw:D_FF + (c + 1) * cw], preferred_element_type=F32)
        act = (gate * jax.nn.sigmoid(gate) * up).astype(BF16)
        acc = acc + jnp.dot(act, wout_ref[c * cw:(c + 1) * cw, :], preferred_element_type=F32)
    y = x + 0.5 * acc
    if with_final:
        y = _rms(y, fn_ref[...])
    o_ref[...] = y


def _ffn(x, norm_g, w_in, w_out, layer, *, mix=None, final_g=None, tm=512, n_chunks=1):
    n = x.shape[0]
    tok = lambda w: pl.BlockSpec((tm, w), lambda i: (i, 0))
    layer_spec = lambda r, c: pl.BlockSpec((None, r, c), lambda i: (layer, 0, 0), pipeline_mode=pl.Buffered(1))
    args, specs = [x], [tok(D_MODEL)]
    if mix is not None:
        yf, yb, g, nb, yc, wo, cn = mix
        args += [yf, yb, g, nb, yc, wo, cn.reshape(1, C_DIM), _group_mat(C_DIM, HEAD_DIM, 1.0 / HEAD_DIM)]
        specs += [tok(A_DIM), tok(A_DIM), tok(A_DIM), tok(B_DIM), tok(C_DIM),
                  _const_spec((D_MODEL, D_MODEL)), _const_spec((1, C_DIM)), _const_spec((C_DIM, C_DIM))]
    args += [norm_g.reshape(1, D_MODEL), w_in, w_out]
    specs += [_const_spec((1, D_MODEL)), layer_spec(D_MODEL, 2 * D_FF), layer_spec(D_FF, D_MODEL)]
    if final_g is not None:
        args.append(final_g.reshape(1, D_MODEL))
        specs.append(_const_spec((1, D_MODEL)))
    body = functools.partial(_ffn_body, n_chunks=n_chunks, with_mix=mix is not None,
                             with_final=final_g is not None)
    return pl.pallas_call(
        body, grid=(n // tm,), in_specs=specs, out_specs=tok(D_MODEL),
        out_shape=jax.ShapeDtypeStruct((n, D_MODEL), F32),
        compiler_params=_cparams(("parallel",)), name="ffn")(*args)


def _halo_specs(tm, width, n_tok):
    per = tm // SUBLANES
    last = n_tok // SUBLANES - 1
    prev = pl.BlockSpec((SUBLANES, width), lambda i: (jnp.maximum(i * per - 1, 0), 0))
    nxt = pl.BlockSpec((SUBLANES, width), lambda i: (jnp.minimum((i + 1) * per, last), 0))
    return prev, nxt


def _shifted(p, row_before, row_after, blocks_per_seq):
    tm = p.shape[0]
    j = pl.program_id(0) % blocks_per_seq
    hp = jnp.where(j == 0, 0.0, row_before)
    hn = jnp.where(j == blocks_per_seq - 1, 0.0, row_after)
    row = lax.broadcasted_iota(jnp.int32, p.shape, 0)
    prev = jnp.where(row == 0, hp, pltpu.roll(p, 1, axis=0))
    nxt = jnp.where(row == tm - 1, hn, pltpu.roll(p, tm - 1, axis=0))
    return prev, nxt


def _mixin_body(x_ref, xp_ref, xn_ref, g_ref, w_ref, mu_ref, wlo_ref, w0a0_ref, gup_ref, kk_ref, ka_ref, rk_ref,
                hsum_ref, cw_ref, cb_ref,
                pb_o, r_o, v_o, kk_o, g_o, lw0_o, lw1_o, kd0_o, kd1_o, b0_o, b1_o, bv0_o, bv1_o, hv_o, hx1_o, hx2_o,
                *, blocks_per_seq):
    a_cols = slice(0, A_PAD)
    c_cols = slice(A_PAD + B_COLS, A_PAD + B_COLS + C_COLS)
    h = _rms(x_ref[...], g_ref[...]).astype(BF16)
    pa = jnp.dot(h, w_ref[:, a_cols], preferred_element_type=F32)
    pb_o[...] = jnp.dot(h, w_ref[:, A_PAD:A_PAD + B_COLS], preferred_element_type=F32).astype(BF16)
    pc = jnp.dot(h, w_ref[:, c_cols], preferred_element_type=F32)
    hh = _rms(jnp.concatenate([xp_ref[...], xn_ref[...]], axis=0), g_ref[...]).astype(BF16)
    ha = jnp.dot(hh, w_ref[:, a_cols], preferred_element_type=F32)
    hc = jnp.dot(hh, w_ref[:, c_cols], preferred_element_type=F32)
    before, after = slice(SUBLANES - 1, SUBLANES), slice(SUBLANES, SUBLANES + 1)

    prev, nxt = _shifted(pc, hc[before], hc[after], blocks_per_seq)
    y = prev * cw_ref[0:1, :] + pc * cw_ref[1:2, :] + nxt * cw_ref[2:3, :] + cb_ref[...]
    hv_o[...] = y[:, 0:C_DIM]
    hx1_o[...] = y[:, C_DIM:2 * C_DIM]
    hx2_o[...] = y[:, 2 * C_DIM:]

    prev, nxt = _shifted(pa, ha[before], ha[after], blocks_per_seq)
    p = pa + mu_ref[...] * (0.5 * (prev + nxt) - pa)
    r = p[:, 0:A_DIM]
    k = p[:, A_DIM:2 * A_DIM]
    v = p[:, 2 * A_DIM:3 * A_DIM]
    lo = p[:, 3 * A_DIM:3 * A_DIM + 128]
    gd = p[:, 3 * A_DIM + 128:3 * A_DIM + 256]
    lane = lax.broadcasted_iota(jnp.int32, lo.shape, 1)
    lo = jnp.where(lane < 2 * A_LORA_W, jnp.tanh(lo), lo)
    wa = _dot3(lo, wlo_ref[...]) + w0a0_ref[...]
    g = _dot3(jax.nn.sigmoid(gd), gup_ref[...])
    hsum = hsum_ref[...]
    kkv = k * kk_ref[...]
    nrm = jnp.sqrt(_dot_lhs2(kkv * kkv, hsum))
    kkv = kkv / jnp.maximum(nrm, 1e-12)
    r_o[...] = r
    v_o[...] = v
    kk_o[...] = kkv
    g_o[...] = g
    for d, (lw_o, kd_o, b_o, bv_o) in enumerate(((lw0_o, kd0_o, b0_o, bv0_o), (lw1_o, kd1_o, b1_o, bv1_o))):
        w_log = -jax.nn.softplus(-wa[:, d * A_DIM:(d + 1) * A_DIM]) - 0.5
        a = jax.nn.sigmoid(wa[:, (2 + d) * A_DIM:(3 + d) * A_DIM])
        kd = k * (1.0 + (a - 1.0) * ka_ref[...])
        lw_o[...] = -jnp.exp(w_log)
        kd_o[...] = kd
        b_o[...] = kkv * a
        bv_o[...] = _dot_lhs2(r * kd * rk_ref[...], hsum) * v


def _mixin(x, lp, *, seq_len, tm=512):
    n = x.shape[0]
    tok = lambda w: pl.BlockSpec((tm, w), lambda i: (i, 0))
    prev, nxt = _halo_specs(tm, D_MODEL, n)
    w_pad = lp['w_in_padded']
    mu = jnp.pad(lp['rwkv_mu'], (0, A_PAD - A_COLS)).reshape(1, A_PAD)
    wlo = jnp.zeros((128, 4 * A_DIM), F32)
    for d in range(2):
        wlo = wlo.at[d * A_LORA_W:(d + 1) * A_LORA_W, d * A_DIM:(d + 1) * A_DIM].set(lp['rwkv_w_up'][d])
        wlo = wlo.at[64 + d * A_LORA_A:64 + (d + 1) * A_LORA_A, (2 + d) * A_DIM:(3 + d) * A_DIM].set(lp['rwkv_a_up'][d])
    w0a0 = jnp.concatenate([lp['rwkv_w0'].reshape(-1), lp['rwkv_a0'].reshape(-1)]).reshape(1, 4 * A_DIM)
    gup = jnp.pad(lp['rwkv_g_up'], ((0, 128 - A_LORA_G), (0, 0)))
    consts = [lp['norm_mix'].reshape(1, D_MODEL), w_pad, mu, wlo, w0a0, gup, lp['rwkv_k_k'].reshape(1, A_DIM),
              lp['rwkv_k_a'].reshape(1, A_DIM), lp['rwkv_r_k'].reshape(1, A_DIM), _group_mat(A_DIM, HEAD_DIM, 1.0),
              jnp.pad(lp['hy_conv_w'], ((0, SUBLANES - 3), (0, 0))), lp['hy_conv_b'].reshape(1, C_COLS)]
    body = functools.partial(_mixin_body, blocks_per_seq=seq_len // tm)
    outs = pl.pallas_call(
        body, grid=(n // tm,),
        in_specs=[tok(D_MODEL), prev, nxt] + [_const_spec(c.shape) for c in consts],
        out_specs=[tok(B_COLS)] + [tok(A_DIM)] * 12 + [tok(C_DIM)] * 3,
        out_shape=[jax.ShapeDtypeStruct((n, B_COLS), BF16)] + [jax.ShapeDtypeStruct((n, A_DIM), F32)] * 15,
        compiler_params=_cparams(("parallel",)), name="mixin")(x, x, x, *consts)
    return outs[0], outs[1:13], outs[13:16]


def _wkv_setup(r, v, kk, lw, kd, b, reverse):
    tb = r.shape[0]
    c_len = WKV_CHUNK
    pos = lax.broadcasted_iota(jnp.int32, (tb, A_DIM), 0) % c_len
    cum = lw
    sh = 1
    while sh < c_len:
        if reverse:
            cum = cum + jnp.where(pos < c_len - sh, pltpu.roll(cum, tb - sh, axis=0), 0.0)
        else:
            cum = cum + jnp.where(pos >= sh, pltpu.roll(cum, sh, axis=0), 0.0)
        sh *= 2
    e_neg = jnp.exp(-cum)
    return dict(cum=cum, rq=r * jnp.exp(cum), kq=kk * jnp.exp(cum - lw), kbar=kd * e_neg, bbar=b * e_neg, v=v)


def _wkv_body(rf, vf, kkf, lwf, kdf, bf, bvf, rb, vb, kkb, lwb, kdb, bb, bvb, lng_ref, lnb_ref, pavg_ref,
              of_ref, ob_ref, sf_ref, sb_ref):
    @pl.when(pl.program_id(1) == 0)
    def _():
        sf_ref[...] = jnp.zeros_like(sf_ref)
        sb_ref[...] = jnp.zeros_like(sb_ref)

    c_len = WKV_CHUNK
    tb = rf.shape[0]
    nc = tb // c_len
    dirs = (_wkv_setup(rf[...], vf[...], kkf[...], lwf[...], kdf[...], bf[...], False),
            _wkv_setup(rb[...], vb[...], kkb[...], lwb[...], kdb[...], bb[...], True))
    s_refs, o_refs, bvs = (sf_ref, sb_ref), (of_ref, ob_ref), (bvf, bvb)

    t_idx = lax.broadcasted_iota(jnp.int32, (c_len, A_DIM), 0)
    s_idx = lax.broadcasted_iota(jnp.int32, (c_len, A_DIM), 1) % c_len
    incl = (s_idx <= t_idx, s_idx >= t_idx)
    strict = (s_idx < t_idx, s_idx > t_idx)
    eye = (s_idx == t_idx).astype(F32)
    bd_mask = (lax.broadcasted_iota(jnp.int32, (A_DIM, A_DIM), 0) // HEAD_DIM
               == lax.broadcasted_iota(jnp.int32, (A_DIM, A_DIM), 1) // HEAD_DIM)
    zero_bf = jnp.zeros((A_DIM, A_DIM), BF16)

    def stack_hl(x_hl):
        return tuple(jnp.where(bd_mask, jnp.concatenate([part] * A_HEADS, axis=0), zero_bf) for part in x_hl)

    def stack(x):
        return stack_hl(_split(x))

    def mm_hl(a_hl, b_hl, nt=False):
        bh, bl = b_hl
        ah, al = a_hl
        m = ah.shape[0]
        dg = _dot_nt if nt else _dot
        p = dg(jnp.concatenate([ah, al], axis=0), bh)
        return p[0:m] + p[m:] + dg(ah, bl)

    def mm(a, b_hl, nt=False):
        return mm_hl(_split(a), b_hl, nt)

    items = [(d, c) for c in range(nc) for d in (0, 1)]
    ch = {}
    for d, c in items:
        sl = slice(c * c_len, (c + 1) * c_len)
        q = dirs[d]
        ch[d, c] = dict(rq=q['rq'][sl], kq=q['kq'][sl], kbar=q['kbar'][sl], bbar=q['bbar'][sl], v=q['v'][sl])
    for key in items:
        e = ch[key]
        kst, bst = stack(e['kbar']), stack(e['bbar'])
        acat = mm(jnp.concatenate([e['rq'], e['kq']], axis=0),
                  (jnp.concatenate([kst[0], bst[0]], axis=0), jnp.concatenate([kst[1], bst[1]], axis=0)),
                  nt=True)
        d = key[0]
        e['a_rv'] = jnp.concatenate([jnp.where(incl[d], acat[0:c_len, 0:A_DIM], 0.0),
                                     jnp.where(strict[d], acat[c_len:, 0:A_DIM], 0.0)], axis=0)
        e['a_rb'] = jnp.where(incl[d], acat[0:c_len, A_DIM:], 0.0)
        e['a_kb'] = jnp.where(strict[d], acat[c_len:, A_DIM:], 0.0)
        e['a_kb_hl'] = _split(e['a_kb'])
    half = 1
    zero_c = jnp.zeros((c_len, A_DIM), BF16)
    while half < c_len:
        in_block = (t_idx // (2 * half)) == (s_idx // (2 * half))
        lo_t, lo_s = t_idx % (2 * half) < half, s_idx % (2 * half) < half
        off = (in_block & ~lo_t & lo_s, in_block & lo_t & ~lo_s)
        if half == 1:
            for key in items:
                ch[key]['tinv'] = eye - jnp.where(off[key[0]], ch[key]['a_kb'], 0.0)
        else:
            for key in items:
                e = ch[key]
                e['tinv_hl'] = _split(e['tinv'])
                a_off_hl = tuple(jnp.where(off[key[0]], part, zero_c) for part in e['a_kb_hl'])
                e['tmp'] = mm_hl(e['tinv_hl'], stack_hl(a_off_hl))
            for key in items:
                e = ch[key]
                e['tinv'] = e['tinv'] - mm(e['tmp'], stack_hl(e['tinv_hl']))
        half *= 2
    for key in items:
        e = ch[key]
        av = mm(e['a_rv'], stack(e['v']))
        e['o_intra'], e['akkv'] = av[0:c_len], av[c_len:]
    for key in items:
        e = ch[key]
        t_hl = _split(e['tinv'])
        e['w_t'] = mm_hl(t_hl, stack(e['kq']))
        e['u0'] = mm_hl(t_hl, stack(e['akkv']))
    outs = ([None] * nc, [None] * nc)
    for step in range(nc):
        cur = [(0, step), (1, nc - 1 - step)]
        xs = {}
        for d, c in cur:
            e = ch[d, c]
            e['s_old'] = s_refs[d][...]
            xs[d] = mm(jnp.concatenate([e['rq'], e['w_t']], axis=0), _split(e['s_old']), nt=True)
        for d, c in cur:
            e = ch[d, c]
            u = -(xs[d][c_len:] + e['u0'])
            edge = c * c_len if d else (c + 1) * c_len - 1
            e_end = jnp.exp(dirs[d]['cum'][edge:edge + 1, :])
            vu_h, vu_l = _split(jnp.concatenate([e['v'], u], axis=0))
            kb_h, kb_l = _split(jnp.concatenate([e['kbar'] * e_end, e['bbar'] * e_end], axis=0))
            ds = _dot_tn(jnp.concatenate([vu_h, vu_l, vu_h], axis=0), jnp.concatenate([kb_h, kb_h, kb_l], axis=0))
            s_refs[d][...] = e['s_old'] * e_end + jnp.where(bd_mask, ds, 0.0)
            outs[d][c] = xs[d][0:c_len] + e['o_intra'] + mm(e['a_rb'], stack(u))
    lng, lnb, pavg = lng_ref[...], lnb_ref[...], pavg_ref[...]
    for d in (0, 1):
        o = jnp.concatenate(outs[d], axis=0)
        mu = _dot_lhs2(o, pavg)
        dev = o - mu
        var = _dot_lhs2(dev * dev, pavg)
        o_refs[d][...] = dev * lax.rsqrt(var + GN_EPS) * lng + lnb + bvs[d][...]


def _wkv(prep, lp, *, batch, seq_len, tb=512):
    r, v, kk, _, lw0, lw1, kd0, kd1, b0, b1, bv0, bv1 = prep
    n = r.shape[0]
    nblk = seq_len // tb
    fwd = pl.BlockSpec((tb, A_DIM), lambda bi, j: (bi * nblk + j, 0))
    bwd = pl.BlockSpec((tb, A_DIM), lambda bi, j: (bi * nblk + nblk - 1 - j, 0))
    consts = [lp['rwkv_ln_g'].reshape(1, A_DIM), lp['rwkv_ln_b'].reshape(1, A_DIM),
              _group_mat(A_DIM, HEAD_DIM, 1.0 / HEAD_DIM)]
    cspec = lambda shape: pl.BlockSpec(shape, lambda bi, j: (0, 0), pipeline_mode=pl.Buffered(1))
    return pl.pallas_call(
        _wkv_body, grid=(batch, nblk),
        in_specs=[fwd] * 7 + [bwd] * 7 + [cspec(c.shape) for c in consts],
        out_specs=[fwd, bwd],
        out_shape=[jax.ShapeDtypeStruct((n, A_DIM), F32)] * 2,
        scratch_shapes=[pltpu.VMEM((A_DIM, A_DIM), F32), pltpu.VMEM((A_DIM, A_DIM), F32)],
        compiler_params=_cparams(("parallel", "arbitrary")), name="wkv")(
            r, v, kk, lw0, kd0, b0, bv0, r, v, kk, lw1, kd1, b1, bv1, *consts)


def _na_tables(rpb, kh):
    col = np.arange(GRID_W)
    col_start = np.clip(col - NA_KW // 2, 0, GRID_W - NA_KW)
    col_ok = (col[None, :] >= col_start[:, None]) & (col[None, :] < col_start[:, None] + NA_KW)
    dx_idx = np.clip(col[None, :] - col[:, None] + NA_KW - 1, 0, 2 * NA_KW - 2)
    dx_hot = ((np.arange(2 * NA_KW - 1)[:, None, None] == dx_idx[None]) & col_ok[None]).astype(np.float32)
    dy_idx = NA_KH_MAX - 1 - np.arange(kh)[:, None] + np.arange(kh)[None, :]
    dy_hot = (dy_idx[:, :, None] == np.arange(2 * NA_KH_MAX - 1)[None, None, :]).astype(np.float32)
    mask_add = np.where(col_ok, 0.0, NEG_BIG).astype(np.float32)[None, None, :, None, :]
    bias = jnp.einsum('ekd,hdx,xqc->ehqkc', jnp.asarray(dy_hot), rpb.astype(F32), jnp.asarray(dx_hot),
                      precision=lax.Precision.HIGHEST)
    return (bias + jnp.asarray(mask_add)).reshape(kh, B_HEADS * GRID_W, kh * GRID_W)


def _na_body(*refs, rows, kh, rq):
    q_ref, k_ref, v_ref = refs[0:3]
    bm_refs = refs[3:3 + rq]
    gn_ref, pavg_ref, o_ref = refs[3 + rq:]
    n_pairs = B_HEADS // 2
    pw = 2 * HEAD_DIM
    second = lax.broadcasted_iota(jnp.int32, (GRID_W, pw), 1) >= HEAD_DIM
    wins, ss = [], []
    for i in range(rq):
        r = pl.program_id(1) * rq + i
        rs = jnp.clip(r - kh // 2, 0, rows - kh)
        start = pl.multiple_of(rs * GRID_W, GRID_W)
        wins.append(start)
        parts = []
        for p in range(n_pairs):
            qp = q_ref[0, i * GRID_W:(i + 1) * GRID_W, p * pw:(p + 1) * pw]
            zero = jnp.zeros_like(qp)
            qs = jnp.concatenate([jnp.where(second, zero, qp), jnp.where(second, qp, zero)], axis=0)
            parts.append(_dot_nt(qs, k_ref[0, pl.ds(start, kh * GRID_W), p * pw:(p + 1) * pw]))
        ss.append(jnp.concatenate(parts, axis=0))
    ps, ls = [], []
    for i in range(rq):
        s = ss[i] * (HEAD_DIM ** -0.5) + bm_refs[i][0]
        m = jnp.max(s, axis=-1, keepdims=True)
        p = jnp.exp(s - m)
        ls.append(jnp.sum(p, axis=-1, keepdims=True))
        ps.append(p.astype(BF16))
    outs = []
    for i in range(rq):
        cols = []
        for p in range(n_pairs):
            rows_p = slice(p * 2 * GRID_W, (p + 1) * 2 * GRID_W)
            pv = _dot(ps[i][rows_p], v_ref[0, pl.ds(wins[i], kh * GRID_W), p * pw:(p + 1) * pw]) / ls[i][rows_p]
            cols.append(jnp.where(second, pv[GRID_W:], pv[0:GRID_W]))
        outs.append(jnp.concatenate(cols, axis=1))
    out = jnp.concatenate(outs, axis=0)
    ms = _dot_lhs2(out * out, pavg_ref[...])
    o_ref[0] = (out * lax.rsqrt(ms + NORM_EPS) * gn_ref[...]).astype(o_ref.dtype)


def _natten(pb, rpb, out_norm, *, batch, seq_len, rq=8):
    rows = seq_len // GRID_W
    kh = min(NA_KH_MAX, rows)
    tables = _na_tables(rpb, kh)
    pb3 = pb.reshape(batch, seq_len, B_COLS)

    def delta_map(i):
        def index(bi, j):
            r = j * rq + i
            return (r - jnp.clip(r - kh // 2, 0, rows - kh), 0, 0)
        return index

    body = functools.partial(_na_body, rows=rows, kh=kh, rq=rq)
    out = pl.pallas_call(
        body, grid=(batch, rows // rq),
        in_specs=[pl.BlockSpec((1, rq * GRID_W, B_DIM), lambda bi, j: (bi, j, 0)),
                  pl.BlockSpec((1, seq_len, B_DIM), lambda bi, j: (bi, 0, 1)),
                  pl.BlockSpec((1, seq_len, B_DIM), lambda bi, j: (bi, 0, 2))]
                 + [pl.BlockSpec((1, B_HEADS * GRID_W, kh * GRID_W), delta_map(i)) for i in range(rq)]
                 + [pl.BlockSpec((1, B_DIM), lambda bi, j: (0, 0)),
                    pl.BlockSpec((B_DIM, B_DIM), lambda bi, j: (0, 0))],
        out_specs=pl.BlockSpec((1, rq * GRID_W, B_DIM), lambda bi, j: (bi, j, 0)),
        out_shape=jax.ShapeDtypeStruct((batch, seq_len, B_DIM), BF16),
        compiler_params=_cparams(("parallel", "arbitrary")), name="natten")(
            pb3, pb3, pb3, *([tables] * rq), out_norm.reshape(1, B_DIM),
            _group_mat(B_DIM, HEAD_DIM, 1.0 / HEAD_DIM))
    return out.reshape(batch * seq_len, B_DIM)


def _hy_filter_body(z_ref, t_ref, w1_ref, b1_ref, w2_ref, b2_ref, w3_ref, fr_ref, dl_ref, k_o, s_o):
    fr = fr_ref[...]
    h = jnp.sin(fr * (_dot3(z_ref[...], w1_ref[...]) + b1_ref[...]))
    h = jnp.sin(fr * (_dot3(h, w2_ref[...]) + b2_ref[...]))
    h = _dot3(h, w3_ref[...])
    tcol = t_ref[...]
    kern = h * jnp.exp(-tcol[:, 0:1] * jnp.abs(dl_ref[...])) * tcol[:, 1:2]
    k_o[...] = kern

    @pl.when(pl.program_id(0) == 0)
    def _():
        s_o[...] = jnp.zeros_like(s_o)

    s_o[...] += jnp.sum(jnp.abs(kern), axis=0, keepdims=True)


def _hy_filter(lp, seq_len, *, rb=512):
    ln = seq_len
    pos = np.concatenate([np.arange(ln), np.array([0]), np.arange(ln - 1, 0, -1)]).astype(np.float64)
    valid = np.ones(2 * ln)
    valid[ln] = 0.0
    t32 = np.linspace(0.0, 1.0, ln, dtype=np.float32)[:, None]
    w32 = (np.float32(2.0 * math.pi) * np.arange(ln, dtype=np.float32)[:, None] / np.float32(ln)).astype(np.float32)
    f32 = np.linspace(1e-4, HY_BANDS - 1, HY_BANDS, dtype=np.float32)[None, :]
    z32 = np.concatenate([t32, np.cos(f32 * w32), -np.sin(f32 * w32)], axis=-1).astype(np.float32)
    idx = pos.astype(np.int64)
    z = np.zeros((2 * ln, 64), np.float32)
    z[:, :HY_EMB] = z32[idx]
    tcol = np.stack([t32[idx, 0], valid.astype(np.float32)], axis=-1)
    max_decay = math.log(HY_TARGET) / HY_FAST_DECAY_PCT
    min_decay = math.log(HY_TARGET) / HY_SLOW_DECAY_PCT
    deltas = np.linspace(min_decay, max_decay, C_DIM, dtype=np.float32)
    dl = jnp.asarray(np.tile(deltas, HY_ORDER)[None, :])
    w1 = jnp.pad(lp['hy_ff_w1'], ((0, 64 - HY_EMB), (0, 0)))
    half = ln // rb
    return pl.pallas_call(
        _hy_filter_body, grid=(2 * ln // rb,),
        in_specs=[pl.BlockSpec((rb, 64), lambda i: (i, 0)), pl.BlockSpec((rb, 2), lambda i: (i, 0)),
                  _const_spec((64, HY_FF)), _const_spec((1, HY_FF)), _const_spec((HY_FF, HY_FF)),
                  _const_spec((1, HY_FF)),
                  pl.BlockSpec((HY_FF, HY_ORDER * C_DIM), lambda i: (0, i // half)),
                  _const_spec((1, HY_FF)), _const_spec((1, HY_ORDER * C_DIM))],
        out_specs=[pl.BlockSpec((rb, HY_ORDER * C_DIM), lambda i: (i, 0)),
                   pl.BlockSpec((1, HY_ORDER * C_DIM), lambda i: (0, 0))],
        out_shape=[jax.ShapeDtypeStruct((2 * ln, HY_ORDER * C_DIM), F32),
                   jax.ShapeDtypeStruct((1, HY_ORDER * C_DIM), F32)],
        compiler_params=_cparams(("arbitrary",)), name="hy_filter")(
            jnp.asarray(z), jnp.asarray(tcol), w1, lp['hy_ff_b1'].reshape(1, HY_FF), lp['hy_ff_w2'],
            lp['hy_ff_b2'].reshape(1, HY_FF), lp['hy_ff_w3'], lp['hy_sin_freq'].reshape(1, HY_FF), dl)


def _dft_split(n):
    n2 = 64
    return n // n2, n2


def _dft_tables(n):
    n1, n2 = _dft_split(n)
    k1 = np.arange(n1)[:, None]
    m1 = np.arange(n1)[None, :]
    ang1 = 2.0 * np.pi * (k1 * m1 % n1) / n1
    f1 = np.stack([np.cos(ang1), -np.sin(ang1)], axis=1).reshape(2 * n1, n1)
    f1inv = np.stack([np.cos(ang1), -np.sin(ang1)], axis=2).reshape(n1, 2 * n1) / n
    eye = np.eye(DFT_ROWS)
    kk1 = np.arange(n1)[:, None, None]
    kk2 = np.arange(n2)[None, :, None]
    nn2 = np.arange(n2)[None, None, :]
    ang = 2.0 * np.pi * ((nn2 * (kk1 + n1 * kk2)) % n) / n
    c, s = np.cos(ang), np.sin(ang)
    g = np.concatenate([np.concatenate([c, s], axis=2), np.concatenate([-s, c], axis=2)], axis=1)
    as_bf = lambda a: jnp.asarray(a.astype(np.float32), dtype=BF16)
    return (as_bf(np.kron(f1, eye)), as_bf(np.kron(f1[:, 0:n1 // 2], eye)),
            as_bf(np.kron(f1inv[0:n1 // 2, :], eye)), as_bf(g))


def _hy_outer_body(f_ref, x_ref, o_ref):
    k1, rows, w = x_ref.shape[1:]
    x = x_ref[0].reshape(k1 * rows, w).astype(BF16)
    y = jnp.dot(f_ref[...], x, preferred_element_type=F32)
    o_ref[0] = y.reshape(y.shape[0] // rows, rows, w)


def _hy_outer(fk, x):
    bsz, k1, n2, w = x.shape
    m1 = fk.shape[0] // DFT_ROWS
    return pl.pallas_call(
        _hy_outer_body, grid=(bsz, n2 // DFT_ROWS),
        in_specs=[pl.BlockSpec(fk.shape, lambda bi, j: (0, 0), pipeline_mode=pl.Buffered(1)),
                  pl.BlockSpec((1, k1, DFT_ROWS, w), lambda bi, j: (bi, 0, j, 0))],
        out_specs=pl.BlockSpec((1, m1, DFT_ROWS, w), lambda bi, j: (bi, 0, j, 0)),
        out_shape=jax.ShapeDtypeStruct((bsz, m1, n2, w), F32),
        compiler_params=_cparams(("parallel", "parallel")), name="hy_outer")(fk, x)


def _hy_spec_body(g_ref, a_ref, inv_ref, h_ref):
    kb = g_ref.shape[0]
    for i in range(kb):
        h = jnp.dot(g_ref[i], a_ref[i].astype(BF16), preferred_element_type=F32) * inv_ref[...]
        h_ref[i] = h.astype(h_ref.dtype)


def _hy_spectrum(g, a, inv_norm, *, kb=8):
    n1, m, w = a.shape
    return pl.pallas_call(
        _hy_spec_body, grid=(n1 // kb,),
        in_specs=[pl.BlockSpec((kb, m, m), lambda i: (i, 0, 0)), pl.BlockSpec((kb, m, w), lambda i: (i, 0, 0)),
                  pl.BlockSpec((1, w), lambda i: (0, 0))],
        out_specs=pl.BlockSpec((kb, m, w), lambda i: (i, 0, 0)),
        out_shape=jax.ShapeDtypeStruct((n1, m, w), BF16),
        compiler_params=_cparams(("parallel",)), name="hy_spectrum")(g, a, inv_norm)


def _hy_conv_body(fk_ref, fki_ref, g_ref, h_ref, u_ref, x_ref, d_ref, o_ref, a_ref, *, kb):
    n1h, n2, c = u_ref.shape[1:]
    n1 = 2 * n1h
    blocks = [slice(j * DFT_ROWS, (j + 1) * DFT_ROWS) for j in range(n2 // DFT_ROWS)]
    fk = fk_ref[...]
    for rows in blocks:
        xb = u_ref[0, :, rows, :].reshape(n1h * DFT_ROWS, c).astype(BF16)
        a_ref[:, rows, :] = jnp.dot(fk, xb, preferred_element_type=F32).reshape(2 * n1, DFT_ROWS, c)

    def group(i, carry):
        ks = [i * kb + t for t in range(kb)]
        xs = [jnp.dot(g_ref[k], a_ref[pl.ds(2 * k, 2)].reshape(2 * n2, c).astype(BF16),
                      preferred_element_type=F32) for k in ks]
        zs = []
        for k, x in zip(ks, xs):
            xr, xi = x[0:n2], x[n2:]
            hr, hi = h_ref[k, 0:n2, :].astype(F32), h_ref[k, n2:, :].astype(F32)
            zs.append(jnp.concatenate([xr * hr - xi * hi, xr * hi + xi * hr], axis=0))
        for k, z in zip(ks, zs):
            a_ref[pl.ds(2 * k, 2)] = _dot_tn(g_ref[k], z).reshape(2, n2, c)
        return carry

    lax.fori_loop(0, n1 // kb, group, 0)
    fki = fki_ref[...]
    for rows in blocks:
        ab = a_ref[:, rows, :].reshape(2 * n1 * DFT_ROWS, c).astype(BF16)
        y = jnp.dot(fki, ab, preferred_element_type=F32).reshape(n1h, DFT_ROWS, c)
        o_ref[0, :, rows, :] = x_ref[0, :, rows, :] * (y + u_ref[0, :, rows, :] * d_ref[...])


def _hy_conv(fk_half, fk_inv, g, hspec, u, xg, dskip, order, *, kb=8):
    bsz, n1h, n2, c = u.shape
    n1 = 2 * n1h
    seq = lambda mode: pl.BlockSpec((1, n1h, n2, c), lambda bi: (bi, 0, 0, 0), pipeline_mode=mode)
    body = functools.partial(_hy_conv_body, kb=kb)
    return pl.pallas_call(
        body, grid=(bsz,),
        in_specs=[_const_spec(fk_half.shape), _const_spec(fk_inv.shape), _const_spec(g.shape),
                  pl.BlockSpec((n1, 2 * n2, c), lambda bi: (0, 0, order), pipeline_mode=pl.Buffered(1)),
                  seq(pl.Buffered(1)), seq(pl.Buffered(1)), _const_spec((1, c))],
        out_specs=pl.BlockSpec((1, n1h, n2, c), lambda bi: (bi, 0, 0, 0)),
        out_shape=jax.ShapeDtypeStruct(u.shape, F32),
        scratch_shapes=[pltpu.VMEM((2 * n1, n2, c), F32)],
        compiler_params=pltpu.CompilerParams(dimension_semantics=("parallel",), vmem_limit_bytes=HY_VMEM_LIMIT),
        name="hy_conv")(fk_half, fk_inv, g, hspec, u, xg, dskip)


def _hyena(vxx, lp, *, batch, seq_len):
    ln = seq_len
    n = 2 * ln
    n1, n2 = _dft_split(n)
    fk_full, fk_half, fk_inv, g = _dft_tables(n)
    v, x1, x2 = vxx
    kern, l1 = _hy_filter(lp, ln)
    ka = _hy_outer(fk_full, kern.reshape(1, n1, n2, HY_ORDER * C_DIM))
    hspec = _hy_spectrum(g, ka.reshape(n1, 2 * n2, HY_ORDER * C_DIM), 1.0 / l1)
    view = lambda t: t.reshape(batch, n1 // 2, n2, C_DIM)

    def conv_gate(u, xg, order):
        return _hy_conv(fk_half, fk_inv, g, hspec, u, xg, lp['hy_bias'][order:order + 1], order)

    z = conv_gate(view(v), view(x1), 0)
    y = conv_gate(z, view(x2), 1)
    return y.reshape(batch * ln, C_DIM)


def _trunk(x, params, final_norm):
    batch, seq_len, _ = x.shape
    x = x.reshape(batch * seq_len, D_MODEL)
    w = params['w_in']
    params = dict(params, w_in_padded=jnp.concatenate(
        [w[:, :, :A_COLS], jnp.zeros((DEPTH, D_MODEL, A_PAD - A_COLS), F32), w[:, :, A_COLS:]], axis=2).astype(BF16))
    w1 = (params['ffn1_w_in'].astype(BF16), params['ffn1_w_out'].astype(BF16))
    w2 = (params['ffn2_w_in'].astype(BF16), params['ffn2_w_out'].astype(BF16))
    mix = None
    for layer in range(DEPTH):
        lp = {name: arr[layer] for name, arr in params.items()}
        if mix is not None:
            x = _ffn(x, params['norm_ffn2'][layer - 1], *w2, layer - 1, mix=mix)
        x = _ffn(x, lp['norm_ffn1'], *w1, layer)
        pb, prep, vxx = _mixin(x, lp, seq_len=seq_len)
        yf, yb = _wkv(prep, lp, batch=batch, seq_len=seq_len)
        nb = _natten(pb, lp['na_rpb'], lp['na_out_norm'], batch=batch, seq_len=seq_len)
        yc = _hyena(vxx, lp, batch=batch, seq_len=seq_len)
        mix = (yf, yb, prep[3], nb, yc, lp['w_out'].astype(BF16), lp['hy_out_norm'])
    x = _ffn(x, params['norm_ffn2'][DEPTH - 1], *w2, DEPTH - 1, mix=mix, final_g=final_norm)
    return x.reshape(batch, seq_len, D_MODEL)


def kernel(x_prompt, x_sample, norm_ffn1, ffn1_w_in, ffn1_w_out, norm_mix, w_in, rwkv_mu, rwkv_w0, rwkv_w_up, rwkv_a0, rwkv_a_up, rwkv_g_up, rwkv_k_k, rwkv_k_a, rwkv_r_k, rwkv_ln_g, rwkv_ln_b, na_rpb, na_out_norm, hy_conv_w, hy_conv_b, hy_ff_w1, hy_ff_b1, hy_ff_w2, hy_ff_b2, hy_ff_w3, hy_sin_freq, hy_bias, hy_out_norm, w_out, norm_ffn2, ffn2_w_in, ffn2_w_out, final_norm):
    params = {
        'norm_ffn1': norm_ffn1, 'ffn1_w_in': ffn1_w_in, 'ffn1_w_out': ffn1_w_out,
        'norm_mix': norm_mix, 'w_in': w_in,
        'rwkv_mu': rwkv_mu, 'rwkv_w0': rwkv_w0, 'rwkv_w_up': rwkv_w_up, 'rwkv_a0': rwkv_a0,
        'rwkv_a_up': rwkv_a_up, 'rwkv_g_up': rwkv_g_up, 'rwkv_k_k': rwkv_k_k, 'rwkv_k_a': rwkv_k_a,
        'rwkv_r_k': rwkv_r_k.reshape(DEPTH, A_DIM), 'rwkv_ln_g': rwkv_ln_g, 'rwkv_ln_b': rwkv_ln_b,
        'na_rpb': na_rpb, 'na_out_norm': na_out_norm,
        'hy_conv_w': hy_conv_w, 'hy_conv_b': hy_conv_b, 'hy_ff_w1': hy_ff_w1, 'hy_ff_b1': hy_ff_b1,
        'hy_ff_w2': hy_ff_w2, 'hy_ff_b2': hy_ff_b2, 'hy_ff_w3': hy_ff_w3, 'hy_sin_freq': hy_sin_freq,
        'hy_bias': hy_bias, 'hy_out_norm': hy_out_norm,
        'w_out': w_out, 'norm_ffn2': norm_ffn2, 'ffn2_w_in': ffn2_w_in, 'ffn2_w_out': ffn2_w_out,
    }
    return (_trunk(x_prompt, params, final_norm), _trunk(x_sample, params, final_norm))
```

```python
import functools
import math

import numpy as np
import jax
import jax.numpy as jnp
from jax import lax
from jax.experimental import pallas as pl
from jax.experimental.pallas import tpu as pltpu

F32 = jnp.float32
BF16 = jnp.bfloat16

D_MODEL = 1024
DEPTH = 4
GRID_W = 64
HEAD_DIM = 64
A_HEADS = 4
A_DIM = A_HEADS * HEAD_DIM
A_LORA_W = 32
A_LORA_A = 32
A_LORA_G = 64
B_HEADS = 8
B_DIM = B_HEADS * HEAD_DIM
NA_KH_MAX = 8
NA_KW = 16
C_DIM = D_MODEL - A_DIM - B_DIM
C_GROUPS = C_DIM // HEAD_DIM
HY_ORDER = 2
HY_BANDS = 16
HY_EMB = 1 + 2 * HY_BANDS
HY_FF = 64
HY_FAST_DECAY_PCT = 0.3
HY_SLOW_DECAY_PCT = 1.5
HY_TARGET = 1e-2
D_FF = 2816
NORM_EPS = 1e-5
GN_EPS = 64e-5
A_COLS = 3 * A_DIM + 2 * A_LORA_W + 2 * A_LORA_A + A_LORA_G
A_PAD = 1024
B_COLS = 3 * B_DIM
C_COLS = (HY_ORDER + 1) * C_DIM

SUBLANES = 8
VMEM_LIMIT = 56 * 1024 * 1024
HY_VMEM_LIMIT = 60 * 1024 * 1024
WKV_CHUNK = 64
NEG_BIG = -1e30
DFT_ROWS = SUBLANES


def _cparams(sem):
    return pltpu.CompilerParams(dimension_semantics=sem, vmem_limit_bytes=VMEM_LIMIT)


def _const_spec(shape):
    nd = len(shape)
    return pl.BlockSpec(shape, lambda *_: (0,) * nd, pipeline_mode=pl.Buffered(1))


def _dot(a, b):
    return jnp.dot(a.astype(BF16), b.astype(BF16), preferred_element_type=F32)


def _dot_nt(a, b):
    return lax.dot_general(a.astype(BF16), b.astype(BF16), (((1,), (1,)), ((), ())),
                           preferred_element_type=F32)


def _dot_tn(a, b):
    return lax.dot_general(a.astype(BF16), b.astype(BF16), (((0,), (0,)), ((), ())),
                           preferred_element_type=F32)


def _split(a):
    hi = a.astype(BF16)
    lo = (a - hi.astype(F32)).astype(BF16)
    return hi, lo


def _dot_lhs2(a, b_bf16):
    hi, lo = _split(a)
    return (jnp.dot(hi, b_bf16, preferred_element_type=F32)
            + jnp.dot(lo, b_bf16, preferred_element_type=F32))


def _dot3(a, b):
    ah, al = _split(a)
    bh, bl = _split(b)
    return (jnp.dot(ah, bh, preferred_element_type=F32)
            + jnp.dot(al, bh, preferred_element_type=F32)
            + jnp.dot(ah, bl, preferred_element_type=F32))


def _rms(x, g):
    return x * lax.rsqrt(jnp.mean(x * x, axis=-1, keepdims=True) + NORM_EPS) * g


def _group_mat(n, group, value):
    idx = np.arange(n) // group
    return jnp.asarray((idx[:, None] == idx[None, :]).astype(np.float32) * value, dtype=BF16)


def _ffn_body(*refs, n_chunks, with_mix, with_final):
    it = iter(refs)
    x_ref = next(it)
    if with_mix:
        yf_ref, yb_ref, g_ref, nb_ref, yc_ref, wo_ref, cn_ref, pavg_ref = (next(it) for _ in range(8))
    ng_ref, win_ref, wout_ref = next(it), next(it), next(it)
    if with_final:
        fn_ref = next(it)
    o_ref = next(it)

    x = x_ref[...]
    if with_mix:
        ya = (yf_ref[...] + yb_ref[...]) * g_ref[...]
        yc = yc_ref[...]
        ms = _dot_lhs2(yc * yc, pavg_ref[...])
        yc = yc * lax.rsqrt(ms + NORM_EPS) * cn_ref[...]
        mixed = jnp.concatenate([ya.astype(BF16), nb_ref[...], yc.astype(BF16)], axis=1)
        x = x + jnp.dot(mixed, wo_ref[...], preferred_element_type=F32)
    h = _rms(x, ng_ref[...]).astype(BF16)
    cw = D_FF // n_chunks
    acc = jnp.zeros_like(x)
    for c in range(n_chunks):
        gate = jnp.dot(h, win_ref[:, c * cw:(c + 1) * cw], preferred_element_type=F32)
        up = jnp.dot(h, win_ref[:, D_FF + c * cw:D_FF + (c + 1) * cw], preferred_element_type=F32)
        act = (gate * jax.nn.sigmoid(gate) * up).astype(BF16)
        acc = acc + jnp.dot(act, wout_ref[c * cw:(c + 1) * cw, :], preferred_element_type=F32)
    y = x + 0.5 * acc
    if with_final:
        y = _rms(y, fn_ref[...])
    o_ref[...] = y


def _ffn(x, norm_g, w_in, w_out, layer, *, mix=None, final_g=None, tm=512, n_chunks=1):
    n = x.shape[0]
    tok = lambda w: pl.BlockSpec((tm, w), lambda i: (i, 0))
    layer_spec = lambda r, c: pl.BlockSpec((None, r, c), lambda i: (layer, 0, 0), pipeline_mode=pl.Buffered(1))
    args, specs = [x], [tok(D_MODEL)]
    if mix is not None:
        yf, yb, g, nb, yc, wo, cn = mix
        args += [yf, yb, g, nb, yc, wo, cn.reshape(1, C_DIM), _group_mat(C_DIM, HEAD_DIM, 1.0 / HEAD_DIM)]
        specs += [tok(A_DIM), tok(A_DIM), tok(A_DIM), tok(B_DIM), tok(C_DIM),
                  _const_spec((D_MODEL, D_MODEL)), _const_spec((1, C_DIM)), _const_spec((C_DIM, C_DIM))]
    args += [norm_g.reshape(1, D_MODEL), w_in, w_out]
    specs += [_const_spec((1, D_MODEL)), layer_spec(D_MODEL, 2 * D_FF), layer_spec(D_FF, D_MODEL)]
    if final_g is not None:
        args.append(final_g.reshape(1, D_MODEL))
        specs.append(_const_spec((1, D_MODEL)))
    body = functools.partial(_ffn_body, n_chunks=n_chunks, with_mix=mix is not None,
                             with_final=final_g is not None)
    return pl.pallas_call(
        body, grid=(n // tm,), in_specs=specs, out_specs=tok(D_MODEL),
        out_shape=jax.ShapeDtypeStruct((n, D_MODEL), F32),
        compiler_params=_cparams(("parallel",)), name="ffn")(*args)


def _halo_specs(tm, width, n_tok):
    per = tm // SUBLANES
    last = n_tok // SUBLANES - 1
    prev = pl.BlockSpec((SUBLANES, width), lambda i: (jnp.maximum(i * per - 1, 0), 0))
    nxt = pl.BlockSpec((SUBLANES, width), lambda i: (jnp.minimum((i + 1) * per, last), 0))
    return prev, nxt


def _shifted(p, row_before, row_after, blocks_per_seq):
    tm = p.shape[0]
    j = pl.program_id(0) % blocks_per_seq
    hp = jnp.where(j == 0, 0.0, row_before)
    hn = jnp.where(j == blocks_per_seq - 1, 0.0, row_after)
    row = lax.broadcasted_iota(jnp.int32, p.shape, 0)
    prev = jnp.where(row == 0, hp, pltpu.roll(p, 1, axis=0))
    nxt = jnp.where(row == tm - 1, hn, pltpu.roll(p, tm - 1, axis=0))
    return prev, nxt


def _mixin_body(x_ref, xp_ref, xn_ref, g_ref, w_ref, mu_ref, wlo_ref, w0a0_ref, gup_ref, kk_ref, ka_ref, rk_ref,
                hsum_ref, cw_ref, cb_ref,
                pb_o, r_o, v_o, kk_o, g_o, lw0_o, lw1_o, kd0_o, kd1_o, b0_o, b1_o, bv0_o, bv1_o, hv_o, hx1_o, hx2_o,
                *, blocks_per_seq):
    a_cols = slice(0, A_PAD)
    c_cols = slice(A_PAD + B_COLS, A_PAD + B_COLS + C_COLS)
    h = _rms(x_ref[...], g_ref[...]).astype(BF16)
    pa = jnp.dot(h, w_ref[:, a_cols], preferred_element_type=F32)
    pb_o[...] = jnp.dot(h, w_ref[:, A_PAD:A_PAD + B_COLS], preferred_element_type=F32).astype(BF16)
    pc = jnp.dot(h, w_ref[:, c_cols], preferred_element_type=F32)
    hh = _rms(jnp.concatenate([xp_ref[...], xn_ref[...]], axis=0), g_ref[...]).astype(BF16)
    ha = jnp.dot(hh, w_ref[:, a_cols], preferred_element_type=F32)
    hc = jnp.dot(hh, w_ref[:, c_cols], preferred_element_type=F32)
    before, after = slice(SUBLANES - 1, SUBLANES), slice(SUBLANES, SUBLANES + 1)

    prev, nxt = _shifted(pc, hc[before], hc[after], blocks_per_seq)
    y = prev * cw_ref[0:1, :] + pc * cw_ref[1:2, :] + nxt * cw_ref[2:3, :] + cb_ref[...]
    hv_o[...] = y[:, 0:C_DIM]
    hx1_o[...] = y[:, C_DIM:2 * C_DIM]
    hx2_o[...] = y[:, 2 * C_DIM:]

    prev, nxt = _shifted(pa, ha[before], ha[after], blocks_per_seq)
    p = pa + mu_ref[...] * (0.5 * (prev + nxt) - pa)
    r = p[:, 0:A_DIM]
    k = p[:, A_DIM:2 * A_DIM]
    v = p[:, 2 * A_DIM:3 * A_DIM]
    lo = p[:, 3 * A_DIM:3 * A_DIM + 128]
    gd = p[:, 3 * A_DIM + 128:3 * A_DIM + 256]
    lane = lax.broadcasted_iota(jnp.int32, lo.shape, 1)
    lo = jnp.where(lane < 2 * A_LORA_W, jnp.tanh(lo), lo)
    wa = _dot3(lo, wlo_ref[...]) + w0a0_ref[...]
    g = _dot3(jax.nn.sigmoid(gd), gup_ref[...])
    hsum = hsum_ref[...]
    kkv = k * kk_ref[...]
    nrm = jnp.sqrt(_dot_lhs2(kkv * kkv, hsum))
    kkv = kkv / jnp.maximum(nrm, 1e-12)
    r_o[...] = r
    v_o[...] = v
    kk_o[...] = kkv
    g_o[...] = g
    for d, (lw_o, kd_o, b_o, bv_o) in enumerate(((lw0_o, kd0_o, b0_o, bv0_o), (lw1_o, kd1_o, b1_o, bv1_o))):
        w_log = -jax.nn.softplus(-wa[:, d * A_DIM:(d + 1) * A_DIM]) - 0.5
        a = jax.nn.sigmoid(wa[:, (2 + d) * A_DIM:(3 + d) * A_DIM])
        kd = k * (1.0 + (a - 1.0) * ka_ref[...])
        lw_o[...] = -jnp.exp(w_log)
        kd_o[...] = kd
        b_o[...] = kkv * a
        bv_o[...] = _dot_lhs2(r * kd * rk_ref[...], hsum) * v


def _mixin(x, lp, *, seq_len, tm=512):
    n = x.shape[0]
    tok = lambda w: pl.BlockSpec((tm, w), lambda i: (i, 0))
    prev, nxt = _halo_specs(tm, D_MODEL, n)
    w_pad = lp['w_in_padded']
    mu = jnp.pad(lp['rwkv_mu'], (0, A_PAD - A_COLS)).reshape(1, A_PAD)
    wlo = jnp.zeros((128, 4 * A_DIM), F32)
    for d in range(2):
        wlo = wlo.at[d * A_LORA_W:(d + 1) * A_LORA_W, d * A_DIM:(d + 1) * A_DIM].set(lp['rwkv_w_up'][d])
        wlo = wlo.at[64 + d * A_LORA_A:64 + (d + 1) * A_LORA_A, (2 + d) * A_DIM:(3 + d) * A_DIM].set(lp['rwkv_a_up'][d])
    w0a0 = jnp.concatenate([lp['rwkv_w0'].reshape(-1), lp['rwkv_a0'].reshape(-1)]).reshape(1, 4 * A_DIM)
    gup = jnp.pad(lp['rwkv_g_up'], ((0, 128 - A_LORA_G), (0, 0)))
    consts = [lp['norm_mix'].reshape(1, D_MODEL), w_pad, mu, wlo, w0a0, gup, lp['rwkv_k_k'].reshape(1, A_DIM),
              lp['rwkv_k_a'].reshape(1, A_DIM), lp['rwkv_r_k'].reshape(1, A_DIM), _group_mat(A_DIM, HEAD_DIM, 1.0),
              jnp.pad(lp['hy_conv_w'], ((0, SUBLANES - 3), (0, 0))), lp['hy_conv_b'].reshape(1, C_COLS)]
    body = functools.partial(_mixin_body, blocks_per_seq=seq_len // tm)
    outs = pl.pallas_call(
        body, grid=(n // tm,),
        in_specs=[tok(D_MODEL), prev, nxt] + [_const_spec(c.shape) for c in consts],
        out_specs=[tok(B_COLS)] + [tok(A_DIM)] * 12 + [tok(C_DIM)] * 3,
        out_shape=[jax.ShapeDtypeStruct((n, B_COLS), BF16)] + [jax.ShapeDtypeStruct((n, A_DIM), F32)] * 15,
        compiler_params=_cparams(("parallel",)), name="mixin")(x, x, x, *consts)
    return outs[0], outs[1:13], outs[13:16]


def _wkv_setup(r, v, kk, lw, kd, b, reverse):
    tb = r.shape[0]
    c_len = WKV_CHUNK
    pos = lax.broadcasted_iota(jnp.int32, (tb, A_DIM), 0) % c_len
    cum = lw
    sh = 1
    while sh < c_len:
        if reverse:
            cum = cum + jnp.where(pos < c_len - sh, pltpu.roll(cum, tb - sh, axis=0), 0.0)
        else:
            cum = cum + jnp.where(pos >= sh, pltpu.roll(cum, sh, axis=0), 0.0)
        sh *= 2
    e_neg = jnp.exp(-cum)
    return dict(cum=cum, rq=r * jnp.exp(cum), kq=kk * jnp.exp(cum - lw), kbar=kd * e_neg, bbar=b * e_neg, v=v)


def _wkv_body(rf, vf, kkf, lwf, kdf, bf, bvf, rb, vb, kkb, lwb, kdb, bb, bvb, lng_ref, lnb_ref, pavg_ref,
              of_ref, ob_ref, sf_ref, sb_ref):
    @pl.when(pl.program_id(1) == 0)
    def _():
        sf_ref[...] = jnp.zeros_like(sf_ref)
        sb_ref[...] = jnp.zeros_like(sb_ref)

    c_len = WKV_CHUNK
    tb = rf.shape[0]
    nc = tb // c_len
    dirs = (_wkv_setup(rf[...], vf[...], kkf[...], lwf[...], kdf[...], bf[...], False),
            _wkv_setup(rb[...], vb[...], kkb[...], lwb[...], kdb[...], bb[...], True))
    s_refs, o_refs, bvs = (sf_ref, sb_ref), (of_ref, ob_ref), (bvf, bvb)

    t_idx = lax.broadcasted_iota(jnp.int32, (c_len, A_DIM), 0)
    s_idx = lax.broadcasted_iota(jnp.int32, (c_len, A_DIM), 1) % c_len
    incl = (s_idx <= t_idx, s_idx >= t_idx)
    strict = (s_idx < t_idx, s_idx > t_idx)
    eye = (s_idx == t_idx).astype(F32)
    bd_mask = (lax.broadcasted_iota(jnp.int32, (A_DIM, A_DIM), 0) // HEAD_DIM
               == lax.broadcasted_iota(jnp.int32, (A_DIM, A_DIM), 1) // HEAD_DIM)
    zero_bf = jnp.zeros((A_DIM, A_DIM), BF16)

    def stack_hl(x_hl):
        return tuple(jnp.where(bd_mask, jnp.concatenate([part] * A_HEADS, axis=0), zero_bf) for part in x_hl)

    def stack(x):
        return stack_hl(_split(x))

    def mm_hl(a_hl, b_hl, nt=False):
        bh, bl = b_hl
        ah, al = a_hl
        m = ah.shape[0]
        dg = _dot_nt if nt else _dot
        p = dg(jnp.concatenate([ah, al], axis=0), bh)
        return p[0:m] + p[m:] + dg(ah, bl)

    def mm(a, b_hl, nt=False):
        return mm_hl(_split(a), b_hl, nt)

    items = [(d, c) for c in range(nc) for d in (0, 1)]
    ch = {}
    for d, c in items:
        sl = slice(c * c_len, (c + 1) * c_len)
        q = dirs[d]
        ch[d, c] = dict(rq=q['rq'][sl], kq=q['kq'][sl], kbar=q['kbar'][sl], bbar=q['bbar'][sl], v=q['v'][sl])
    for key in items:
        e = ch[key]
        kst, bst = stack(e['kbar']), stack(e['bbar'])
        acat = mm(jnp.concatenate([e['rq'], e['kq']], axis=0),
                  (jnp.concatenate([kst[0], bst[0]], axis=0), jnp.concatenate([kst[1], bst[1]], axis=0)),
                  nt=True)
        d = key[0]
        e['a_rv'] = jnp.concatenate([jnp.where(incl[d], acat[0:c_len, 0:A_DIM], 0.0),
                                     jnp.where(strict[d], acat[c_len:, 0:A_DIM], 0.0)], axis=0)
        e['a_rb'] = jnp.where(incl[d], acat[0:c_len, A_DIM:], 0.0)
        e['a_kb'] = jnp.where(strict[d], acat[c_len:, A_DIM:], 0.0)
        e['a_kb_hl'] = _split(e['a_kb'])
    half = 1
    zero_c = jnp.zeros((c_len, A_DIM), BF16)
    while half < c_len:
        in_block = (t_idx // (2 * half)) == (s_idx // (2 * half))
        lo_t, lo_s = t_idx % (2 * half) < half, s_idx % (2 * half) < half
        off = (in_block & ~lo_t & lo_s, in_block & lo_t & ~lo_s)
        if half == 1:
            for key in items:
                ch[key]['tinv'] = eye - jnp.where(off[key[0]], ch[key]['a_kb'], 0.0)
        else:
            for key in items:
                e = ch[key]
                e['tinv_hl'] = _split(e['tinv'])
                a_off_hl = tuple(jnp.where(off[key[0]], part, zero_c) for part in e['a_kb_hl'])
                e['tmp'] = mm_hl(e['tinv_hl'], stack_hl(a_off_hl))
            for key in items:
                e = ch[key]
                e['tinv'] = e['tinv'] - mm(e['tmp'], stack_hl(e['tinv_hl']))
        half *= 2
    for key in items:
        e = ch[key]
        av = mm(e['a_rv'], stack(e['v']))
        e['o_intra'], e['akkv'] = av[0:c_len], av[c_len:]
    for key in items:
        e = ch[key]
        t_hl = _split(e['tinv'])
        e['w_t'] = mm_hl(t_hl, stack(e['kq']))
        e['u0'] = mm_hl(t_hl, stack(e['akkv']))
    outs = ([None] * nc, [None] * nc)
    for step in range(nc):
        cur = [(0, step), (1, nc - 1 - step)]
        xs = {}
        for d, c in cur:
            e = ch[d, c]
            e['s_old'] = s_refs[d][...]
            xs[d] = mm(jnp.concatenate([e['rq'], e['w_t']], axis=0), _split(e['s_old']), nt=True)
        for d, c in cur:
            e = ch[d, c]
            u = -(xs[d][c_len:] + e['u0'])
            edge = c * c_len if d else (c + 1) * c_len - 1
            e_end = jnp.exp(dirs[d]['cum'][edge:edge + 1, :])
            vu_h, vu_l = _split(jnp.concatenate([e['v'], u], axis=0))
            kb_h, kb_l = _split(jnp.concatenate([e['kbar'] * e_end, e['bbar'] * e_end], axis=0))
            ds = _dot_tn(jnp.concatenate([vu_h, vu_l, vu_h], axis=0), jnp.concatenate([kb_h, kb_h, kb_l], axis=0))
            s_refs[d][...] = e['s_old'] * e_end + jnp.where(bd_mask, ds, 0.0)
            outs[d][c] = xs[d][0:c_len] + e['o_intra'] + mm(e['a_rb'], stack(u))
    lng, lnb, pavg = lng_ref[...], lnb_ref[...], pavg_ref[...]
    for d in (0, 1):
        o = jnp.concatenate(outs[d], axis=0)
        mu = _dot_lhs2(o, pavg)
        dev = o - mu
        var = _dot_lhs2(dev * dev, pavg)
        o_refs[d][...] = dev * lax.rsqrt(var + GN_EPS) * lng + lnb + bvs[d][...]


def _wkv(prep, lp, *, batch, seq_len, tb=512):
    r, v, kk, _, lw0, lw1, kd0, kd1, b0, b1, bv0, bv1 = prep
    n = r.shape[0]
    nblk = seq_len // tb
    fwd = pl.BlockSpec((tb, A_DIM), lambda bi, j: (bi * nblk + j, 0))
    bwd = pl.BlockSpec((tb, A_DIM), lambda bi, j: (bi * nblk + nblk - 1 - j, 0))
    consts = [lp['rwkv_ln_g'].reshape(1, A_DIM), lp['rwkv_ln_b'].reshape(1, A_DIM),
              _group_mat(A_DIM, HEAD_DIM, 1.0 / HEAD_DIM)]
    cspec = lambda shape: pl.BlockSpec(shape, lambda bi, j: (0, 0), pipeline_mode=pl.Buffered(1))
    return pl.pallas_call(
        _wkv_body, grid=(batch, nblk),
        in_specs=[fwd] * 7 + [bwd] * 7 + [cspec(c.shape) for c in consts],
        out_specs=[fwd, bwd],
        out_shape=[jax.ShapeDtypeStruct((n, A_DIM), F32)] * 2,
        scratch_shapes=[pltpu.VMEM((A_DIM, A_DIM), F32), pltpu.VMEM((A_DIM, A_DIM), F32)],
        compiler_params=_cparams(("parallel", "arbitrary")), name="wkv")(
            r, v, kk, lw0, kd0, b0, bv0, r, v, kk, lw1, kd1, b1, bv1, *consts)


def _na_tables(rpb, kh):
    col = np.arange(GRID_W)
    col_start = np.clip(col - NA_KW // 2, 0, GRID_W - NA_KW)
    col_ok = (col[None, :] >= col_start[:, None]) & (col[None, :] < col_start[:, None] + NA_KW)
    dx_idx = np.clip(col[None, :] - col[:, None] + NA_KW - 1, 0, 2 * NA_KW - 2)
    dx_hot = ((np.arange(2 * NA_KW - 1)[:, None, None] == dx_idx[None]) & col_ok[None]).astype(np.float32)
    dy_idx = NA_KH_MAX - 1 - np.arange(kh)[:, None] + np.arange(kh)[None, :]
    dy_hot = (dy_idx[:, :, None] == np.arange(2 * NA_KH_MAX - 1)[None, None, :]).astype(np.float32)
    mask_add = np.where(col_ok, 0.0, NEG_BIG).astype(np.float32)[None, None, :, None, :]
    bias = jnp.einsum('ekd,hdx,xqc->ehqkc', jnp.asarray(dy_hot), rpb.astype(F32), jnp.asarray(dx_hot),
                      precision=lax.Precision.HIGHEST)
    return (bias + jnp.asarray(mask_add)).reshape(kh, B_HEADS * GRID_W, kh * GRID_W)


def _na_body(*refs, rows, kh, rq):
    q_ref, k_ref, v_ref = refs[0:3]
    bm_refs = refs[3:3 + rq]
    gn_ref, pavg_ref, o_ref = refs[3 + rq:]
    n_pairs = B_HEADS // 2
    pw = 2 * HEAD_DIM
    second = lax.broadcasted_iota(jnp.int32, (GRID_W, pw), 1) >= HEAD_DIM
    wins, ss = [], []
    for i in range(rq):
        r = pl.program_id(1) * rq + i
        rs = jnp.clip(r - kh // 2, 0, rows - kh)
        start = pl.multiple_of(rs * GRID_W, GRID_W)
        wins.append(start)
        parts = []
        for p in range(n_pairs):
            qp = q_ref[0, i * GRID_W:(i + 1) * GRID_W, p * pw:(p + 1) * pw]
            zero = jnp.zeros_like(qp)
            qs = jnp.concatenate([jnp.where(second, zero, qp), jnp.where(second, qp, zero)], axis=0)
            parts.append(_dot_nt(qs, k_ref[0, pl.ds(start, kh * GRID_W), p * pw:(p + 1) * pw]))
        ss.append(jnp.concatenate(parts, axis=0))
    ps, ls = [], []
    for i in range(rq):
        s = ss[i] * (HEAD_DIM ** -0.5) + bm_refs[i][0]
        m = jnp.max(s, axis=-1, keepdims=True)
        p = jnp.exp(s - m)
        ls.append(jnp.sum(p, axis=-1, keepdims=True))
        ps.append(p.astype(BF16))
    outs = []
    for i in range(rq):
        cols = []
        for p in range(n_pairs):
            rows_p = slice(p * 2 * GRID_W, (p + 1) * 2 * GRID_W)
            pv = _dot(ps[i][rows_p], v_ref[0, pl.ds(wins[i], kh * GRID_W), p * pw:(p + 1) * pw]) / ls[i][rows_p]
            cols.append(jnp.where(second, pv[GRID_W:], pv[0:GRID_W]))
        outs.append(jnp.concatenate(cols, axis=1))
    out = jnp.concatenate(outs, axis=0)
    ms = _dot_lhs2(out * out, pavg_ref[...])
    o_ref[0] = (out * lax.rsqrt(ms + NORM_EPS) * gn_ref[...]).astype(o_ref.dtype)


def _natten(pb, rpb, out_norm, *, batch, seq_len, rq=8):
    rows = seq_len // GRID_W
    kh = min(NA_KH_MAX, rows)
    tables = _na_tables(rpb, kh)
    pb3 = pb.reshape(batch, seq_len, B_COLS)

    def delta_map(i):
        def index(bi, j):
            r = j * rq + i
            return (r - jnp.clip(r - kh // 2, 0, rows - kh), 0, 0)
        return index

    body = functools.partial(_na_body, rows=rows, kh=kh, rq=rq)
    out = pl.pallas_call(
        body, grid=(batch, rows // rq),
        in_specs=[pl.BlockSpec((1, rq * GRID_W, B_DIM), lambda bi, j: (bi, j, 0)),
                  pl.BlockSpec((1, seq_len, B_DIM), lambda bi, j: (bi, 0, 1)),
                  pl.BlockSpec((1, seq_len, B_DIM), lambda bi, j: (bi, 0, 2))]
                 + [pl.BlockSpec((1, B_HEADS * GRID_W, kh * GRID_W), delta_map(i)) for i in range(rq)]
                 + [pl.BlockSpec((1, B_DIM), lambda bi, j: (0, 0)),
                    pl.BlockSpec((B_DIM, B_DIM), lambda bi, j: (0, 0))],
        out_specs=pl.BlockSpec((1, rq * GRID_W, B_DIM), lambda bi, j: (bi, j, 0)),
        out_shape=jax.ShapeDtypeStruct((batch, seq_len, B_DIM), BF16),
        compiler_params=_cparams(("parallel", "arbitrary")), name="natten")(
            pb3, pb3, pb3, *([tables] * rq), out_norm.reshape(1, B_DIM),
            _group_mat(B_DIM, HEAD_DIM, 1.0 / HEAD_DIM))
    return out.reshape(batch * seq_len, B_DIM)


def _hy_filter_body(z_ref, t_ref, w1_ref, b1_ref, w2_ref, b2_ref, w3_ref, fr_ref, dl_ref, k_o, s_o):
    fr = fr_ref[...]
    h = jnp.sin(fr * (_dot3(z_ref[...], w1_ref[...]) + b1_ref[...]))
    h = jnp.sin(fr * (_dot3(h, w2_ref[...]) + b2_ref[...]))
    h = _dot3(h, w3_ref[...])
    tcol = t_ref[...]
    kern = h * jnp.exp(-tcol[:, 0:1] * jnp.abs(dl_ref[...])) * tcol[:, 1:2]
    k_o[...] = kern

    @pl.when(pl.program_id(0) == 0)
    def _():
        s_o[...] = jnp.zeros_like(s_o)

    s_o[...] += jnp.sum(jnp.abs(kern), axis=0, keepdims=True)


def _hy_filter(lp, seq_len, *, rb=512):
    ln = seq_len
    pos = np.concatenate([np.arange(ln), np.array([0]), np.arange(ln - 1, 0, -1)]).astype(np.float64)
    valid = np.ones(2 * ln)
    valid[ln] = 0.0
    t32 = np.linspace(0.0, 1.0, ln, dtype=np.float32)[:, None]
    w32 = (np.float32(2.0 * math.pi) * np.arange(ln, dtype=np.float32)[:, None] / np.float32(ln)).astype(np.float32)
    f32 = np.linspace(1e-4, HY_BANDS - 1, HY_BANDS, dtype=np.float32)[None, :]
    z32 = np.concatenate([t32, np.cos(f32 * w32), -np.sin(f32 * w32)], axis=-1).astype(np.float32)
    idx = pos.astype(np.int64)
    z = np.zeros((2 * ln, 64), np.float32)
    z[:, :HY_EMB] = z32[idx]
    tcol = np.stack([t32[idx, 0], valid.astype(np.float32)], axis=-1)
    max_decay = math.log(HY_TARGET) / HY_FAST_DECAY_PCT
    min_decay = math.log(HY_TARGET) / HY_SLOW_DECAY_PCT
    deltas = np.linspace(min_decay, max_decay, C_DIM, dtype=np.float32)
    dl = jnp.asarray(np.tile(deltas, HY_ORDER)[None, :])
    w1 = jnp.pad(lp['hy_ff_w1'], ((0, 64 - HY_EMB), (0, 0)))
    half = ln // rb
    return pl.pallas_call(
        _hy_filter_body, grid=(2 * ln // rb,),
        in_specs=[pl.BlockSpec((rb, 64), lambda i: (i, 0)), pl.BlockSpec((rb, 2), lambda i: (i, 0)),
                  _const_spec((64, HY_FF)), _const_spec((1, HY_FF)), _const_spec((HY_FF, HY_FF)),
                  _const_spec((1, HY_FF)),
                  pl.BlockSpec((HY_FF, HY_ORDER * C_DIM), lambda i: (0, i // half)),
                  _const_spec((1, HY_FF)), _const_spec((1, HY_ORDER * C_DIM))],
        out_specs=[pl.BlockSpec((rb, HY_ORDER * C_DIM), lambda i: (i, 0)),
                   pl.BlockSpec((1, HY_ORDER * C_DIM), lambda i: (0, 0))],
        out_shape=[jax.ShapeDtypeStruct((2 * ln, HY_ORDER * C_DIM), F32),
                   jax.ShapeDtypeStruct((1, HY_ORDER * C_DIM), F32)],
        compiler_params=_cparams(("arbitrary",)), name="hy_filter")(
            jnp.asarray(z), jnp.asarray(tcol), w1, lp['hy_ff_b1'].reshape(1, HY_FF), lp['hy_ff_w2'],
            lp['hy_ff_b2'].reshape(1, HY_FF), lp['hy_ff_w3'], lp['hy_sin_freq'].reshape(1, HY_FF), dl)


def _dft_split(n):
    n2 = 64
    return n // n2, n2


def _dft_tables(n):
    n1, n2 = _dft_split(n)
    k1 = np.arange(n1)[:, None]
    m1 = np.arange(n1)[None, :]
    ang1 = 2.0 * np.pi * (k1 * m1 % n1) / n1
    f1 = np.stack([np.cos(ang1), -np.sin(ang1)], axis=1).reshape(2 * n1, n1)
    f1inv = np.stack([np.cos(ang1), -np.sin(ang1)], axis=2).reshape(n1, 2 * n1) / n
    eye = np.eye(DFT_ROWS)
    kk1 = np.arange(n1)[:, None, None]
    kk2 = np.arange(n2)[None, :, None]
    nn2 = np.arange(n2)[None, None, :]
    ang = 2.0 * np.pi * ((nn2 * (kk1 + n1 * kk2)) % n) / n
    c, s = np.cos(ang), np.sin(ang)
    g = np.concatenate([np.concatenate([c, s], axis=2), np.concatenate([-s, c], axis=2)], axis=1)
    as_bf = lambda a: jnp.asarray(a.astype(np.float32), dtype=BF16)
    return (as_bf(np.kron(f1, eye)), as_bf(np.kron(f1[:, 0:n1 // 2], eye)),
            as_bf(np.kron(f1inv[0:n1 // 2, :], eye)), as_bf(g))


def _hy_outer_body(f_ref, x_ref, o_ref):
    k1, rows, w = x_ref.shape[1:]
    x = x_ref[0].reshape(k1 * rows, w).astype(BF16)
    y = jnp.dot(f_ref[...], x, preferred_element_type=F32)
    o_ref[0] = y.reshape(y.shape[0] // rows, rows, w)


def _hy_outer(fk, x):
    bsz, k1, n2, w = x.shape
    m1 = fk.shape[0] // DFT_ROWS
    return pl.pallas_call(
        _hy_outer_body, grid=(bsz, n2 // DFT_ROWS),
        in_specs=[pl.BlockSpec(fk.shape, lambda bi, j: (0, 0), pipeline_mode=pl.Buffered(1)),
                  pl.BlockSpec((1, k1, DFT_ROWS, w), lambda bi, j: (bi, 0, j, 0))],
        out_specs=pl.BlockSpec((1, m1, DFT_ROWS, w), lambda bi, j: (bi, 0, j, 0)),
        out_shape=jax.ShapeDtypeStruct((bsz, m1, n2, w), F32),
        compiler_params=_cparams(("parallel", "parallel")), name="hy_outer")(fk, x)


def _hy_spec_body(g_ref, a_ref, inv_ref, h_ref):
    kb = g_ref.shape[0]
    for i in range(kb):
        h = jnp.dot(g_ref[i], a_ref[i].astype(BF16), preferred_element_type=F32) * inv_ref[...]
        h_ref[i] = h.astype(h_ref.dtype)


def _hy_spectrum(g, a, inv_norm, *, kb=8):
    n1, m, w = a.shape
    return pl.pallas_call(
        _hy_spec_body, grid=(n1 // kb,),
        in_specs=[pl.BlockSpec((kb, m, m), lambda i: (i, 0, 0)), pl.BlockSpec((kb, m, w), lambda i: (i, 0, 0)),
                  pl.BlockSpec((1, w), lambda i: (0, 0))],
        out_specs=pl.BlockSpec((kb, m, w), lambda i: (i, 0, 0)),
        out_shape=jax.ShapeDtypeStruct((n1, m, w), BF16),
        compiler_params=_cparams(("parallel",)), name="hy_spectrum")(g, a, inv_norm)


def _hy_conv_body(fk_ref, fki_ref, g_ref, h_ref, u_ref, x_ref, d_ref, o_ref, a_ref, *, kb):
    n1h, n2, c = u_ref.shape[1:]
    n1 = 2 * n1h
    blocks = [slice(j * DFT_ROWS, (j + 1) * DFT_ROWS) for j in range(n2 // DFT_ROWS)]
    fk = fk_ref[...]
    for rows in blocks:
        xb = u_ref[0, :, rows, :].reshape(n1h * DFT_ROWS, c).astype(BF16)
        a_ref[:, rows, :] = jnp.dot(fk, xb, preferred_element_type=F32).reshape(2 * n1, DFT_ROWS, c)

    def group(i, carry):
        ks = [i * kb + t for t in range(kb)]
        xs = [jnp.dot(g_ref[k], a_ref[pl.ds(2 * k, 2)].reshape(2 * n2, c).astype(BF16),
                      preferred_element_type=F32) for k in ks]
        zs = []
        for k, x in zip(ks, xs):
            xr, xi = x[0:n2], x[n2:]
            hr, hi = h_ref[k, 0:n2, :].astype(F32), h_ref[k, n2:, :].astype(F32)
            zs.append(jnp.concatenate([xr * hr - xi * hi, xr * hi + xi * hr], axis=0))
        for k, z in zip(ks, zs):
            a_ref[pl.ds(2 * k, 2)] = _dot_tn(g_ref[k], z).reshape(2, n2, c)
        return carry

    lax.fori_loop(0, n1 // kb, group, 0)
    fki = fki_ref[...]
    for rows in blocks:
        ab = a_ref[:, rows, :].reshape(2 * n1 * DFT_ROWS, c).astype(BF16)
        y = jnp.dot(fki, ab, preferred_element_type=F32).reshape(n1h, DFT_ROWS, c)
        o_ref[0, :, rows, :] = x_ref[0, :, rows, :] * (y + u_ref[0, :, rows, :] * d_ref[...])


def _hy_conv(fk_half, fk_inv, g, hspec, u, xg, dskip, order, *, kb=8):
    bsz, n1h, n2, c = u.shape
    n1 = 2 * n1h
    seq = lambda mode: pl.BlockSpec((1, n1h, n2, c), lambda bi: (bi, 0, 0, 0), pipeline_mode=mode)
    body = functools.partial(_hy_conv_body, kb=kb)
    return pl.pallas_call(
        body, grid=(bsz,),
        in_specs=[_const_spec(fk_half.shape), _const_spec(fk_inv.shape), _const_spec(g.shape),
                  pl.BlockSpec((n1, 2 * n2, c), lambda bi: (0, 0, order), pipeline_mode=pl.Buffered(1)),
                  seq(pl.Buffered(1)), seq(pl.Buffered(1)), _const_spec((1, c))],
        out_specs=pl.BlockSpec((1, n1h, n2, c), lambda bi: (bi, 0, 0, 0)),
        out_shape=jax.ShapeDtypeStruct(u.shape, F32),
        scratch_shapes=[pltpu.VMEM((2 * n1, n2, c), F32)],
        compiler_params=pltpu.CompilerParams(dimension_semantics=("parallel",), vmem_limit_bytes=HY_VMEM_LIMIT),
        name="hy_conv")(fk_half, fk_inv, g, hspec, u, xg, dskip)


def _hyena(vxx, lp, *, batch, seq_len):
    ln = seq_len
    n = 2 * ln
    n1, n2 = _dft_split(n)
    fk_full, fk_half, fk_inv, g = _dft_tables(n)
    v, x1, x2 = vxx
    kern, l1 = _hy_filter(lp, ln)
    ka = _hy_outer(fk_full, kern.reshape(1, n1, n2, HY_ORDER * C_DIM))
    hspec = _hy_spectrum(g, ka.reshape(n1, 2 * n2, HY_ORDER * C_DIM), 1.0 / l1)
    view = lambda t: t.reshape(batch, n1 // 2, n2, C_DIM)

    def conv_gate(u, xg, order):
        return _hy_conv(fk_half, fk_inv, g, hspec, u, xg, lp['hy_bias'][order:order + 1], order)

    z = conv_gate(view(v), view(x1), 0)
    y = conv_gate(z, view(x2), 1)
    return y.reshape(batch * ln, C_DIM)


def _trunk(x, params, final_norm):
    batch, seq_len, _ = x.shape
    x = x.reshape(batch * seq_len, D_MODEL)
    w = params['w_in']
    params = dict(params, w_in_padded=jnp.concatenate(
        [w[:, :, :A_COLS], jnp.zeros((DEPTH, D_MODEL, A_PAD - A_COLS), F32), w[:, :, A_COLS:]], axis=2).astype(BF16))
    w1 = (params['ffn1_w_in'].astype(BF16), params['ffn1_w_out'].astype(BF16))
    w2 = (params['ffn2_w_in'].astype(BF16), params['ffn2_w_out'].astype(BF16))
    mix = None
    for layer in range(DEPTH):
        lp = {name: arr[layer] for name, arr in params.items()}
        if mix is not None:
            x = _ffn(x, params['norm_ffn2'][layer - 1], *w2, layer - 1, mix=mix)
        x = _ffn(x, lp['norm_ffn1'], *w1, layer)
        pb, prep, vxx = _mixin(x, lp, seq_len=seq_len)
        yf, yb = _wkv(prep, lp, batch=batch, seq_len=seq_len)
        nb = _natten(pb, lp['na_rpb'], lp['na_out_norm'], batch=batch, seq_len=seq_len)
        yc = _hyena(vxx, lp, batch=batch, seq_len=seq_len)
        mix = (yf, yb, prep[3], nb, yc, lp['w_out'].astype(BF16), lp['hy_out_norm'])
    x = _ffn(x, params['norm_ffn2'][DEPTH - 1], *w2, DEPTH - 1, mix=mix, final_g=final_norm)
    return x.reshape(batch, seq_len, D_MODEL)


def kernel(x_prompt, x_sample, norm_ffn1, ffn1_w_in, ffn1_w_out, norm_mix, w_in, rwkv_mu, rwkv_w0, rwkv_w_up, rwkv_a0, rwkv_a_up, rwkv_g_up, rwkv_k_k, rwkv_k_a, rwkv_r_k, rwkv_ln_g, rwkv_ln_b, na_rpb, na_out_norm, hy_conv_w, hy_conv_b, hy_ff_w1, hy_ff_b1, hy_ff_w2, hy_ff_b2, hy_ff_w3, hy_sin_freq, hy_bias, hy_out_norm, w_out, norm_ffn2, ffn2_w_in, ffn2_w_out, final_norm):
    params = {
        'norm_ffn1': norm_ffn1, 'ffn1_w_in': ffn1_w_in, 'ffn1_w_out': ffn1_w_out,
        'norm_mix': norm_mix, 'w_in': w_in,
        'rwkv_mu': rwkv_mu, 'rwkv_w0': rwkv_w0, 'rwkv_w_up': rwkv_w_up, 'rwkv_a0': rwkv_a0,
        'rwkv_a_up': rwkv_a_up, 'rwkv_g_up': rwkv_g_up, 'rwkv_k_k': rwkv_k_k, 'rwkv_k_a': rwkv_k_a,
        'rwkv_r_k': rwkv_r_k.reshape(DEPTH, A_DIM), 'rwkv_ln_g': rwkv_ln_g, 'rwkv_ln_b': rwkv_ln_b,
        'na_rpb': na_rpb, 'na_out_norm': na_out_norm,
        'hy_conv_w': hy_conv_w, 'hy_conv_b': hy_conv_b, 'hy_ff_w1': hy_ff_w1, 'hy_ff_b1': hy_ff_b1,
        'hy_ff_w2': hy_ff_w2, 'hy_ff_b2': hy_ff_b2, 'hy_ff_w3': hy_ff_w3, 'hy_sin_freq': hy_sin_freq,
        'hy_bias': hy_bias, 'hy_out_norm': hy_out_norm,
        'w_out': w_out, 'norm_ffn2': norm_ffn2, 'ffn2_w_in': ffn2_w_in, 'ffn2_w_out': ffn2_w_out,
    }
    return (_trunk(x_prompt, params, final_norm), _trunk(x_sample, params, final_norm))
```
